```python
import jax, jax.numpy as jnp
from jax import lax
import numpy as np

D_MODEL = 2048
BATCH = 1
SEQ = 16384
DEPTH = 4

CONV_WIDTH = D_MODEL // 2
CONV_K = 3
N_HEADS = 16
N_KV_GROUPS = 4
HEADS_PER_GROUP = N_HEADS // N_KV_GROUPS
HEAD_DIM = D_MODEL // N_HEADS
KV_WIDTH = N_KV_GROUPS * HEAD_DIM
ROPE_DIM = HEAD_DIM // 4
ROPE_THETA = 500000.0
CMP_BLOCK = 32
CMP_STRIDE = 16
CMP_HIDDEN = 2 * HEAD_DIM
SEL_BLOCK = 64
SEL_TOP = 16
WINDOW = 512
Q_BLOCK = 128
D_FF = 4 * D_MODEL
NORM_EPS = 1e-6
NEG = -1e30
FORCE = 1e30
MIX_SPLITS = (CONV_WIDTH, CONV_WIDTH, CONV_WIDTH, N_HEADS * HEAD_DIM, 6 * KV_WIDTH, 3 * N_HEADS, D_MODEL, D_MODEL)
IN_WIDTH = 3 * CONV_WIDTH + N_HEADS * HEAD_DIM + 6 * KV_WIDTH + 3 * N_HEADS + 2 * D_MODEL

kernel_name = "hybrid_shortconv_nsa_adaln_block"


def rms_norm(x, g):
    xf = x.astype(jnp.float32)
    y = xf * lax.rsqrt(jnp.mean(xf * xf, axis=-1, keepdims=True) + NORM_EPS)
    return (y * g.astype(jnp.float32)).astype(x.dtype)


def rope_tables(positions):
    inv_freq = ROPE_THETA ** (-jnp.arange(0, ROPE_DIM, 2, dtype=jnp.float32) / ROPE_DIM)
    ang = positions.astype(jnp.float32)[..., None] * inv_freq
    return jnp.cos(ang), jnp.sin(ang)


def partial_rope(x, cos, sin):
    half = ROPE_DIM // 2
    xr = x[..., :ROPE_DIM].astype(jnp.float32)
    x1, x2 = xr[..., :half], xr[..., half:]
    c = cos[:, :, None, :]
    s = sin[:, :, None, :]
    rot = jnp.concatenate([x1 * c - x2 * s, x2 * c + x1 * s], axis=-1).astype(x.dtype)
    return jnp.concatenate([rot, x[..., ROPE_DIM:]], axis=-1)


def short_conv_mixer(b_gate, c_gate, xa, conv_w, w_out):
    u = c_gate * xa
    z = lax.conv_general_dilated(u, conv_w[:, None, :].astype(u.dtype), window_strides=(1,),
                                 padding=[(CONV_K - 1, 0)], dimension_numbers=('NWC', 'WIO', 'NWC'),
                                 feature_group_count=CONV_WIDTH)
    return (b_gate * z) @ w_out


def compress(kv, pe, w1, b1, w2, b2):
    B, S, G, dh = kv.shape
    n_cmp = (S - CMP_BLOCK) // CMP_STRIDE + 1
    idx = jnp.arange(n_cmp)[:, None] * CMP_STRIDE + jnp.arange(CMP_BLOCK)[None, :]
    blocks = kv[:, idx] + pe[None, None, :, None, :]
    blocks = blocks.transpose(0, 1, 3, 2, 4).reshape(B, n_cmp, G, CMP_BLOCK * dh)
    h = jax.nn.gelu(blocks @ w1 + b1)
    return h @ w2 + b2


def nsa_attention(q, kc, vc, ks, vs, kw, vw, gate):
    B, S, H, dh = q.shape
    G, R = N_KV_GROUPS, HEADS_PER_GROUP
    n_cmp = kc.shape[1]
    n_slc = S // SEL_BLOCK
    top = min(SEL_TOP, n_slc)
    n_qb = S // Q_BLOCK
    scale = dh ** -0.5
    qg = q.reshape(B, S, G, R, dh)
    gate = gate.reshape(B, S, G, R, 3)
    cmp_end = jnp.arange(n_cmp) * CMP_STRIDE + CMP_BLOCK - 1
    cs = jnp.arange(n_cmp)[:, None] * CMP_STRIDE
    ss = jnp.arange(n_slc)[None, :] * SEL_BLOCK
    overlap = jnp.clip(jnp.minimum(cs + CMP_BLOCK, ss + SEL_BLOCK) - jnp.maximum(cs, ss), 0, None).astype(jnp.float32) / CMP_BLOCK
    ks_blk = ks.reshape(B, n_slc, SEL_BLOCK, G, dh).transpose(0, 3, 1, 2, 4)
    vs_blk = vs.reshape(B, n_slc, SEL_BLOCK, G, dh).transpose(0, 3, 1, 2, 4)
    kw_pad = jnp.pad(kw, ((0, 0), (WINDOW, 0), (0, 0), (0, 0)))
    vw_pad = jnp.pad(vw, ((0, 0), (WINDOW, 0), (0, 0), (0, 0)))
    b_ix = jnp.arange(B)[:, None, None, None]
    g_ix = jnp.arange(G)[None, :, None, None]
    blk_ids = jnp.arange(n_slc)

    def one_block(i):
        s0 = i * Q_BLOCK
        t = s0 + jnp.arange(Q_BLOCK)
        qb = lax.dynamic_slice_in_dim(qg, s0, Q_BLOCK, axis=1)
        gb = lax.dynamic_slice_in_dim(gate, s0, Q_BLOCK, axis=1)
        sc = jnp.einsum('btgrd,bngd->bgrtn', qb, kc, preferred_element_type=jnp.float32) * scale
        m_c = cmp_end[None, :] <= t[:, None]
        p_c = jnp.where(m_c, jax.nn.softmax(jnp.where(m_c, sc, NEG), axis=-1), 0.0)
        o_c = jnp.einsum('bgrtn,bngd->btgrd', p_c.astype(vc.dtype), vc)
        imp = jnp.einsum('bgrtn,nj->bgtj', p_c, overlap)
        cur = t // SEL_BLOCK
        forced = (blk_ids[None, :] == cur[:, None]) | (blk_ids[None, :] == 0)
        valid = blk_ids[None, :] <= cur[:, None]
        imp = jnp.where(forced, FORCE, jnp.where(valid, imp, NEG))
        _, sel = lax.top_k(imp, top)
        k_sel = ks_blk[b_ix, g_ix, sel]
        v_sel = vs_blk[b_ix, g_ix, sel]
        kpos = sel[..., None] * SEL_BLOCK + jnp.arange(SEL_BLOCK)
        m_s = (kpos <= t[None, None, :, None, None]).reshape(B, G, 1, Q_BLOCK, top * SEL_BLOCK)
        s_s = jnp.einsum('btgrd,bgtkld->bgrtkl', qb, k_sel, preferred_element_type=jnp.float32)
        s_s = s_s.reshape(B, G, R, Q_BLOCK, top * SEL_BLOCK) * scale
        p_s = jax.nn.softmax(jnp.where(m_s, s_s, NEG), axis=-1).reshape(B, G, R, Q_BLOCK, top, SEL_BLOCK)
        o_s = jnp.einsum('bgrtkl,bgtkld->btgrd', p_s.astype(v_sel.dtype), v_sel)
        kwb = lax.dynamic_slice_in_dim(kw_pad, s0, Q_BLOCK + WINDOW, axis=1)
        vwb = lax.dynamic_slice_in_dim(vw_pad, s0, Q_BLOCK + WINDOW, axis=1)
        wpos = s0 - WINDOW + jnp.arange(Q_BLOCK + WINDOW)
        diff = t[:, None] - wpos[None, :]
        m_w = (diff >= 0) & (diff < WINDOW) & (wpos[None, :] >= 0)
        s_w = jnp.einsum('btgrd,bkgd->bgrtk', qb, kwb, preferred_element_type=jnp.float32) * scale
        p_w = jax.nn.softmax(jnp.where(m_w, s_w, NEG), axis=-1)
        o_w = jnp.einsum('bgrtk,bkgd->btgrd', p_w.astype(vwb.dtype), vwb)
        return gb[..., 0:1] * o_c + gb[..., 1:2] * o_s + gb[..., 2:3] * o_w

    out = lax.map(one_block, jnp.arange(n_qb))
    return out.transpose(1, 0, 2, 3, 4, 5).reshape(B, S, H * dh)


def hybrid_mixer(h, cos, sin, w_in, conv_w, w_conv_out, cmp_pe, cmp_w1, cmp_b1, cmp_w2, cmp_b2, w_nsa_out, w_out):
    B, S, _ = h.shape
    points = [int(p) for p in np.cumsum(MIX_SPLITS)[:-1]]
    bg, cg, xa, q, kv, g_nsa, g_a, g_b = jnp.split(h @ w_in, points, axis=-1)
    y_a = short_conv_mixer(bg, cg, xa, conv_w, w_conv_out)
    q = partial_rope(q.reshape(B, S, N_HEADS, HEAD_DIM), cos, sin)
    kv = kv.reshape(B, S, 6, N_KV_GROUPS, HEAD_DIM)
    kc_raw = partial_rope(kv[:, :, 0], cos, sin)
    vc_raw = kv[:, :, 1]
    ks = partial_rope(kv[:, :, 2], cos, sin)
    vs = kv[:, :, 3]
    kw = partial_rope(kv[:, :, 4], cos, sin)
    vw = kv[:, :, 5]
    kc = compress(kc_raw, cmp_pe[0], cmp_w1[0], cmp_b1[0], cmp_w2[0], cmp_b2[0])
    vc = compress(vc_raw, cmp_pe[1], cmp_w1[1], cmp_b1[1], cmp_w2[1], cmp_b2[1])
    gate = jax.nn.sigmoid(g_nsa).reshape(B, S, N_HEADS, 3)
    y_b = nsa_attention(q, kc, vc, ks, vs, kw, vw, gate) @ w_nsa_out
    merged = jax.nn.sigmoid(g_a) * y_a + jax.nn.sigmoid(g_b) * y_b
    return merged @ w_out


def setup_inputs(seed: int = 0) -> dict:
    key = jax.random.key(seed)
    ks = jax.random.split(key, 20)
    f32 = jnp.float32
    nrm = lambda k, shape, s: jax.random.normal(k, shape, f32) * s
    return {
        "x": nrm(ks[0], (BATCH, SEQ, D_MODEL), 1.0),
        "c": nrm(ks[1], (BATCH, D_MODEL), 1.0),
        "positions": jnp.broadcast_to(jnp.arange(SEQ, dtype=jnp.int32), (BATCH, SEQ)),
        "ada_w": nrm(ks[2], (DEPTH, D_MODEL, 6 * D_MODEL), 0.5 * D_MODEL ** -0.5),
        "ada_b": nrm(ks[3], (DEPTH, 6 * D_MODEL), 0.02),
        "norm_gains": 1.0 + nrm(ks[4], (DEPTH, 4, D_MODEL), 0.05),
        "w_in": nrm(ks[5], (DEPTH, D_MODEL, IN_WIDTH), D_MODEL ** -0.5),
        "conv_w": nrm(ks[6], (DEPTH, CONV_K, CONV_WIDTH), CONV_K ** -0.5),
        "w_conv_out": nrm(ks[7], (DEPTH, CONV_WIDTH, D_MODEL), CONV_WIDTH ** -0.5),
        "cmp_pe": nrm(ks[8], (DEPTH, 2, CMP_BLOCK, HEAD_DIM), 0.1),
        "cmp_w1": nrm(ks[9], (DEPTH, 2, CMP_BLOCK * HEAD_DIM, CMP_HIDDEN), (CMP_BLOCK * HEAD_DIM) ** -0.5),
        "cmp_b1": nrm(ks[10], (DEPTH, 2, CMP_HIDDEN), 0.02),
        "cmp_w2": nrm(ks[11], (DEPTH, 2, CMP_HIDDEN, HEAD_DIM), CMP_HIDDEN ** -0.5),
        "cmp_b2": nrm(ks[12], (DEPTH, 2, HEAD_DIM), 0.02),
        "w_nsa_out": nrm(ks[13], (DEPTH, N_HEADS * HEAD_DIM, D_MODEL), (N_HEADS * HEAD_DIM) ** -0.5),
        "w_out": nrm(ks[14], (DEPTH, D_MODEL, D_MODEL), D_MODEL ** -0.5),
        "w_mlp_up": nrm(ks[15], (DEPTH, D_MODEL, D_FF), D_MODEL ** -0.5),
        "w_mlp_down": nrm(ks[16], (DEPTH, D_FF, D_MODEL), D_FF ** -0.5),
    }


def reference(x, c, positions, ada_w, ada_b, norm_gains, w_in, conv_w, w_conv_out, cmp_pe, cmp_w1, cmp_b1,
              cmp_w2, cmp_b2, w_nsa_out, w_out, w_mlp_up, w_mlp_down):
    cos, sin = rope_tables(positions)
    c_act = jax.nn.silu(c)
    for l in range(DEPTH):
        mod = c_act @ ada_w[l] + ada_b[l]
        sh1, sc1, g1, sh2, sc2, g2 = jnp.split(mod, 6, axis=-1)
        h = rms_norm(x, norm_gains[l, 0]) * (1.0 + sc1[:, None, :]) + sh1[:, None, :]
        y = hybrid_mixer(h, cos, sin, w_in[l], conv_w[l], w_conv_out[l], cmp_pe[l], cmp_w1[l], cmp_b1[l],
                         cmp_w2[l], cmp_b2[l], w_nsa_out[l], w_out[l])
        x = x + g1[:, None, :] * rms_norm(y, norm_gains[l, 1])
        h = rms_norm(x, norm_gains[l, 2]) * (1.0 + sc2[:, None, :]) + sh2[:, None, :]
        u = jnp.square(jax.nn.relu(h @ w_mlp_up[l]))
        x = x + g2[:, None, :] * rms_norm(u @ w_mlp_down[l], norm_gains[l, 3])
    return x
```

```python
import functools

import jax
import jax.numpy as jnp
from jax import lax
from jax.experimental import pallas as pl
from jax.experimental.pallas import tpu as pltpu

F32 = jnp.float32
MXU_DTYPE = jnp.bfloat16

D_MODEL = 2048
CONV_WIDTH = D_MODEL // 2
N_HEADS = 16
N_KV_GROUPS = 4
HEADS_PER_GROUP = N_HEADS // N_KV_GROUPS
HEAD_DIM = D_MODEL // N_HEADS
KV_WIDTH = N_KV_GROUPS * HEAD_DIM
ROPE_DIM = HEAD_DIM // 4
ROPE_THETA = 500000.0
CMP_BLOCK = 32
CMP_STRIDE = 16
CMP_HIDDEN = 2 * HEAD_DIM
SEL_BLOCK = 64
SEL_SHIFT = 6
SEL_TOP = 16
WINDOW = 512
D_FF = 4 * D_MODEL
NORM_EPS = 1e-6
NEG = -1e30
FORCE = 1e30
REMOVED = -3e38
SCALE = HEAD_DIM ** -0.5

COL_Q = 3 * CONV_WIDTH
COL_KV = COL_Q + N_HEADS * HEAD_DIM
COL_GATE = COL_KV + 6 * KV_WIDTH
COL_GA = COL_GATE + 3 * N_HEADS
COL_GB = COL_GA + D_MODEL
GATES_PER_GROUP = 3 * HEADS_PER_GROUP

VMEM_LIMIT_BYTES = 56 * 1024 * 1024


def _params(*semantics):
    return pltpu.CompilerParams(dimension_semantics=semantics, vmem_limit_bytes=VMEM_LIMIT_BYTES)


def _dot(a, b):
    return jnp.dot(a, b, preferred_element_type=F32)


def _dot_nt(a, b):
    return lax.dot_general(a, b, (((1,), (1,)), ((), ())), preferred_element_type=F32)


def _rms(y, gain):
    ms = jnp.mean(y * y, axis=-1, keepdims=True)
    return y * lax.rsqrt(ms + NORM_EPS) * gain


def _adaln_kernel(c_ref, w_ref, b_ref, o_ref):
    c = c_ref[...]
    act = (c * jax.nn.sigmoid(c)).astype(MXU_DTYPE)
    o_ref[0] = _dot(act, w_ref[0].astype(MXU_DTYPE)) + b_ref[0]


def _adaln(c, ada_w, ada_b):
    depth, d, n = ada_w.shape
    tn = 1024
    c8 = jnp.broadcast_to(c, (8, d))
    out = pl.pallas_call(
        _adaln_kernel,
        grid=(depth, n // tn),
        in_specs=[
            pl.BlockSpec((8, d), lambda l, j: (0, 0)),
            pl.BlockSpec((1, d, tn), lambda l, j: (l, 0, j)),
            pl.BlockSpec((1, 1, tn), lambda l, j: (l, 0, j)),
        ],
        out_specs=pl.BlockSpec((1, 8, tn), lambda l, j: (l, 0, j)),
        out_shape=jax.ShapeDtypeStruct((depth, 8, n), F32),
        compiler_params=_params("parallel", "parallel"),
        name="adaln",
    )(c8, ada_w, ada_b.reshape(depth, 1, n))
    return out[:, 0, :]


def _rope_table_kernel(pos_ref, freq_ref, c_ref, s1_ref, s2_ref):
    half = ROPE_DIM // 2
    ang = pos_ref[...].astype(F32) * freq_ref[...]
    lane = lax.broadcasted_iota(jnp.int32, ang.shape, 1)
    cos = jnp.cos(ang)
    sin = jnp.sin(ang)
    c_ref[...] = jnp.where(lane < ROPE_DIM, cos, 1.0)
    s1_ref[...] = jnp.where(lane < half, -sin, 0.0)
    s2_ref[...] = jnp.where((lane >= half) & (lane < ROPE_DIM), sin, 0.0)


def _rope_tables(positions):
    s = positions.shape[-1]
    tm = min(s, 2048)
    half = ROPE_DIM // 2
    inv_freq = ROPE_THETA ** (-jnp.arange(0, ROPE_DIM, 2, dtype=F32) / ROPE_DIM)
    freq = jnp.concatenate([inv_freq, inv_freq, jnp.zeros((HEAD_DIM - 2 * half,), F32)]).reshape(1, HEAD_DIM)
    spec = pl.BlockSpec((tm, HEAD_DIM), lambda i: (i, 0))
    shape = jax.ShapeDtypeStruct((s, HEAD_DIM), F32)
    return pl.pallas_call(
        _rope_table_kernel,
        grid=(s // tm,),
        in_specs=[pl.BlockSpec((tm, 1), lambda i: (i, 0)), pl.BlockSpec((1, HEAD_DIM), lambda i: (0, 0))],
        out_specs=[spec, spec, spec],
        out_shape=[shape, shape, shape],
        compiler_params=_params("parallel"),
        name="rope_tables",
    )(positions.reshape(s, 1), freq)


def _rope(x, c, s1, s2):
    half = ROPE_DIM // 2
    return x * c + pltpu.roll(x, HEAD_DIM - half, axis=1) * s1 + pltpu.roll(x, half, axis=1) * s2


def _norm_mod_kernel(x_ref, g_ref, sc_ref, sh_ref, o_ref):
    y = _rms(x_ref[...], g_ref[...])
    o_ref[...] = (y * (1.0 + sc_ref[...]) + sh_ref[...]).astype(o_ref.dtype)


def _norm_mod(x, gain, sc, sh):
    s, d = x.shape
    tm = min(s, 512)
    vec = pl.BlockSpec((1, d), lambda i: (0, 0))
    return pl.pallas_call(
        _norm_mod_kernel,
        grid=(s // tm,),
        in_specs=[pl.BlockSpec((tm, d), lambda i: (i, 0)), vec, vec, vec],
        out_specs=pl.BlockSpec((tm, d), lambda i: (i, 0)),
        out_shape=jax.ShapeDtypeStruct((s, d), MXU_DTYPE),
        compiler_params=_params("parallel"),
        name="norm_mod",
    )(x, gain, sc, sh)


def _conv_proj_kernel(h_ref, wb_ref, wc_ref, wx_ref, cw_ref, o_ref, carry_ref):
    i = pl.program_id(0)
    j = pl.program_id(1)
    tm = h_ref.shape[0]
    h = h_ref[...]
    u = _dot(h, wc_ref[...]) * _dot(h, wx_ref[...])

    @pl.when(i == 0)
    def _():
        carry_ref[j] = jnp.zeros(carry_ref.shape[1:], F32)

    prev = carry_ref[j]
    carry_ref[j] = u[tm - 8:, :]
    row = lax.broadcasted_iota(jnp.int32, u.shape, 0)
    u1 = jnp.where(row == 0, prev[7:8, :], pltpu.roll(u, 1, axis=0))
    u2 = jnp.where(row == 0, prev[6:7, :], jnp.where(row == 1, prev[7:8, :], pltpu.roll(u, 2, axis=0)))
    cw = cw_ref[...]
    z = cw[2:3, :] * u + cw[1:2, :] * u1 + cw[0:1, :] * u2
    o_ref[...] = (_dot(h, wb_ref[...]) * z).astype(o_ref.dtype)


def _conv_proj(h, w_conv, conv_w):
    s, d = h.shape
    tm = min(s, 1024)
    tn = 256
    nb = CONV_WIDTH // tn
    return pl.pallas_call(
        _conv_proj_kernel,
        grid=(s // tm, nb),
        in_specs=[
            pl.BlockSpec((tm, d), lambda i, j: (i, 0)),
            pl.BlockSpec((d, tn), lambda i, j: (0, j)),
            pl.BlockSpec((d, tn), lambda i, j: (0, j + nb)),
            pl.BlockSpec((d, tn), lambda i, j: (0, j + 2 * nb)),
            pl.BlockSpec((3, tn), lambda i, j: (0, j)),
        ],
        out_specs=pl.BlockSpec((tm, tn), lambda i, j: (i, j)),
        out_shape=jax.ShapeDtypeStruct((s, CONV_WIDTH), MXU_DTYPE),
        scratch_shapes=[pltpu.VMEM((nb, 8, tn), F32)],
        compiler_params=_params("arbitrary", "arbitrary"),
        name="conv_proj",
    )(h, w_conv, w_conv, w_conv, conv_w)


def _head_proj_kernel(h_ref, w_ref, c_ref, s1_ref, s2_ref, o_ref, *, rope_groups):
    j = pl.program_id(1)
    acc = _dot(h_ref[...], w_ref[...])
    heads = [acc[:, r * HEAD_DIM:(r + 1) * HEAD_DIM] for r in range(HEADS_PER_GROUP)]
    is_rope = functools.reduce(jnp.logical_or, [j == r for r in rope_groups])

    @pl.when(is_rope)
    def _():
        c, s1, s2 = c_ref[...], s1_ref[...], s2_ref[...]
        for r, x in enumerate(heads):
            o_ref[r] = _rope(x, c, s1, s2).astype(o_ref.dtype)

    @pl.when(jnp.logical_not(is_rope))
    def _():
        for r, x in enumerate(heads):
            o_ref[r] = x.astype(o_ref.dtype)


def _head_proj(h, w, tables, rope_groups, out_dtype):
    s, d = h.shape
    tm = min(s, 1024)
    tn = HEADS_PER_GROUP * HEAD_DIM
    ng = w.shape[1] // tn
    tab = pl.BlockSpec((tm, HEAD_DIM), lambda i, j: (i, 0))
    return pl.pallas_call(
        functools.partial(_head_proj_kernel, rope_groups=rope_groups),
        grid=(s // tm, ng),
        in_specs=[pl.BlockSpec((tm, d), lambda i, j: (i, 0)), pl.BlockSpec((d, tn), lambda i, j: (0, j)), tab, tab, tab],
        out_specs=pl.BlockSpec((HEADS_PER_GROUP, tm, HEAD_DIM), lambda i, j: (j, i, 0)),
        out_shape=jax.ShapeDtypeStruct((ng * HEADS_PER_GROUP, s, HEAD_DIM), out_dtype),
        compiler_params=_params("parallel", "parallel"),
        name="head_proj",
    )(h, w, *tables)


def _gate_proj_kernel(h_ref, w_ref, o_ref):
    o_ref[...] = jax.nn.sigmoid(_dot(h_ref[...], w_ref[...]))


def _gate_proj(h, w_gate):
    s, d = h.shape
    tm = min(s, 1024)
    n = w_gate.shape[1]
    return pl.pallas_call(
        _gate_proj_kernel,
        grid=(s // tm,),
        in_specs=[pl.BlockSpec((tm, d), lambda i: (i, 0)), pl.BlockSpec((d, n), lambda i: (0, 0))],
        out_specs=pl.BlockSpec((tm, n), lambda i: (i, 0)),
        out_shape=jax.ShapeDtypeStruct((s, n), F32),
        compiler_params=_params("parallel"),
        name="gate_proj",
    )(h, w_gate)


def _compress_kernel(x_ref, pe_ref, w1_ref, b1_ref, w2_ref, b2_ref, o_ref):
    n = o_ref.shape[1]
    k_lo = CMP_STRIDE * HEAD_DIM
    pe = pe_ref[0]
    lo, hi = [], []
    for l in range(CMP_STRIDE):
        rows = x_ref[0, pl.ds(l, n, stride=CMP_STRIDE), :]
        lo.append((rows + pe[l:l + 1, :]).astype(MXU_DTYPE))
        hi.append((rows + pe[CMP_STRIDE + l:CMP_STRIDE + l + 1, :]).astype(MXU_DTYPE))
    t_lo = _dot(jnp.concatenate(lo, axis=1), w1_ref[0, :k_lo, :])
    t_hi = _dot(jnp.concatenate(hi, axis=1), w1_ref[0, k_lo:, :])
    hid = t_lo + pltpu.roll(t_hi, n - 1, axis=0) + b1_ref[0]
    out = _dot(jax.nn.gelu(hid).astype(MXU_DTYPE), w2_ref[0]) + b2_ref[0]
    row = lax.broadcasted_iota(jnp.int32, out.shape, 0)
    o_ref[0] = jnp.where(row == n - 1, 0.0, out).astype(o_ref.dtype)


def _compress(kv_raw, pe, w1, b1, w2, b2):
    nh, s, dh = kv_raw.shape
    n = s // CMP_STRIDE
    g = N_KV_GROUPS
    return pl.pallas_call(
        _compress_kernel,
        grid=(nh,),
        in_specs=[
            pl.BlockSpec((1, s, dh), lambda a: (a, 0, 0)),
            pl.BlockSpec((1, CMP_BLOCK, dh), lambda a: (a // g, 0, 0)),
            pl.BlockSpec((1, CMP_BLOCK * dh, CMP_HIDDEN), lambda a: (a // g, 0, 0)),
            pl.BlockSpec((1, 1, CMP_HIDDEN), lambda a: (a // g, 0, 0)),
            pl.BlockSpec((1, CMP_HIDDEN, dh), lambda a: (a // g, 0, 0)),
            pl.BlockSpec((1, 1, dh), lambda a: (a // g, 0, 0)),
        ],
        out_specs=pl.BlockSpec((1, n, dh), lambda a: (a, 0, 0)),
        out_shape=jax.ShapeDtypeStruct((nh, n, dh), MXU_DTYPE),
        compiler_params=_params("parallel"),
        name="compress",
    )(kv_raw, pe, w1, b1.reshape(2, 1, CMP_HIDDEN), w2, b2.reshape(2, 1, dh))


def _masked_exp(s, mask):
    m = jnp.max(jnp.where(mask, s, NEG), axis=-1, keepdims=True)
    p = jnp.where(mask, jnp.exp(SCALE * (s - m)), 0.0)
    l = jnp.sum(p, axis=-1, keepdims=True)
    inv = jnp.where(l > 0.0, 1.0 / l, 0.0)
    return p, inv


def _cmp_win_kernel(q_ref, kc_ref, vc_ref, kw2_ref, kw1_ref, kw0_ref, vw2_ref, vw1_ref, vw0_ref, ov_ref, gate_ref,
                    ocw_ref, sel_ref, work_ref, *, tq):
    i = pl.program_id(1)
    r_heads = HEADS_PER_GROUP
    s0 = i * tq
    q4 = q_ref[...].reshape(r_heads * tq, HEAD_DIM)
    tok = s0 + lax.broadcasted_iota(jnp.int32, (tq, 1), 0)

    kc = kc_ref[0]
    n_cmp = kc.shape[0]
    s_c = _dot_nt(q4, kc)
    cmp_end = lax.broadcasted_iota(jnp.int32, (1, n_cmp), 1) * CMP_STRIDE + (CMP_BLOCK - 1)
    m_c = cmp_end <= tok
    p_rows, inv_rows, p_sum = [], [], None
    for r in range(r_heads):
        p, inv = _masked_exp(s_c[r * tq:(r + 1) * tq, :], m_c)
        p_rows.append(p.astype(MXU_DTYPE))
        inv_rows.append(inv)
        pn = p * inv
        p_sum = pn if p_sum is None else p_sum + pn
    o_c = _dot(jnp.concatenate(p_rows, axis=0), vc_ref[0])

    ov = ov_ref[...]
    hi = p_sum.astype(MXU_DTYPE)
    rem = p_sum - hi.astype(F32)
    mid = rem.astype(MXU_DTYPE)
    lo = (rem - mid.astype(F32)).astype(MXU_DTYPE)
    imp = _dot(hi, ov) + _dot(mid, ov) + _dot(lo, ov)
    n_slc = imp.shape[1]
    blk = lax.broadcasted_iota(jnp.int32, (1, n_slc), 1)
    cur = tok >> SEL_SHIFT
    forced = (blk == cur) | (blk == 0)
    imp = jnp.where(forced, FORCE, jnp.where(blk <= cur, imp, NEG))

    blk_f = blk.astype(F32)
    work_ref[...] = imp
    for _ in range(min(SEL_TOP, n_slc)):
        w = work_ref[...]
        best = jnp.max(w, axis=-1, keepdims=True)
        idx = jnp.min(jnp.where(w == best, blk_f, float(n_slc)), axis=-1, keepdims=True)
        work_ref[...] = jnp.where(blk_f == idx, REMOVED, w)
    sel_ref[0] = jnp.where(work_ref[...] == REMOVED, 1.0, 0.0).astype(sel_ref.dtype)

    kwin = jnp.concatenate([kw2_ref[0], kw1_ref[0], kw0_ref[0]], axis=0)
    vwin = jnp.concatenate([vw2_ref[0], vw1_ref[0], vw0_ref[0]], axis=0)
    s_w = _dot_nt(q4, kwin)
    wpos = s0 - 2 * tq + lax.broadcasted_iota(jnp.int32, (1, 3 * tq), 1)
    diff = tok - wpos
    m_w = (diff >= 0) & (diff < WINDOW) & (wpos >= 0)
    pw_rows, invw_rows = [], []
    for r in range(r_heads):
        p, inv = _masked_exp(s_w[r * tq:(r + 1) * tq, :], m_w)
        pw_rows.append(p.astype(MXU_DTYPE))
        invw_rows.append(inv)
    o_w = _dot(jnp.concatenate(pw_rows, axis=0), vwin)

    gate = gate_ref[...]
    for r in range(r_heads):
        rows = slice(r * tq, (r + 1) * tq)
        g_cmp = gate[:, 3 * r:3 * r + 1]
        g_win = gate[:, 3 * r + 2:3 * r + 3]
        ocw_ref[:, r * HEAD_DIM:(r + 1) * HEAD_DIM] = (
            g_cmp * (o_c[rows, :] * inv_rows[r]) + g_win * (o_w[rows, :] * invw_rows[r]))


def _overlap_matrix(n_cmp_rows, n_slc):
    cs = jnp.arange(n_cmp_rows)[:, None] * CMP_STRIDE
    ss = jnp.arange(n_slc)[None, :] * SEL_BLOCK
    ov = jnp.clip(jnp.minimum(cs + CMP_BLOCK, ss + SEL_BLOCK) - jnp.maximum(cs, ss), 0, None)
    return (ov.astype(F32) / CMP_BLOCK).astype(MXU_DTYPE)


def _cmp_win(qkv, kvc, gates, s):
    tq = min(s, 256)
    assert WINDOW <= 2 * tq
    g = N_KV_GROUPS
    n_cmp = kvc.shape[1]
    n_slc = s // SEL_BLOCK
    head = lambda base: [
        pl.BlockSpec((1, tq, HEAD_DIM), functools.partial(lambda a, i, d, base: (base + a, jnp.maximum(i - d, 0), 0), d=d, base=base))
        for d in (2, 1, 0)
    ]
    ocw, sel = pl.pallas_call(
        functools.partial(_cmp_win_kernel, tq=tq),
        grid=(g, s // tq),
        in_specs=[
            pl.BlockSpec((HEADS_PER_GROUP, tq, HEAD_DIM), lambda a, i: (a, i, 0)),
            pl.BlockSpec((1, n_cmp, HEAD_DIM), lambda a, i: (a, 0, 0)),
            pl.BlockSpec((1, n_cmp, HEAD_DIM), lambda a, i: (g + a, 0, 0)),
            *head(N_HEADS + 2 * g),
            *head(N_HEADS + 3 * g),
            pl.BlockSpec((n_cmp, n_slc), lambda a, i: (0, 0)),
            pl.BlockSpec((tq, HEAD_DIM), lambda a, i: (i, a)),
        ],
        out_specs=[
            pl.BlockSpec((tq, HEADS_PER_GROUP * HEAD_DIM), lambda a, i: (i, a)),
            pl.BlockSpec((1, tq, n_slc), lambda a, i: (a, i, 0)),
        ],
        out_shape=[
            jax.ShapeDtypeStruct((s, N_HEADS * HEAD_DIM), F32),
            jax.ShapeDtypeStruct((g, s, n_slc), MXU_DTYPE),
        ],
        scratch_shapes=[pltpu.VMEM((tq, n_slc), F32)],
        compiler_params=_params("parallel", "parallel"),
        name="cmp_win",
    )(qkv, kvc, kvc, qkv, qkv, qkv, qkv, qkv, qkv, _overlap_matrix(n_cmp, n_slc), gates)
    return ocw, sel


def _sel_kernel(q_ref, k_ref, v_ref, sel_ref, ocw_ref, gate_ref, o_ref, m_ref, l_ref, acc_ref, *, tq, tk):
    i = pl.program_id(1)
    j = pl.program_id(2)
    r_heads = HEADS_PER_GROUP
    n_slc = sel_ref.shape[2]
    blocks_per_tile = tk // SEL_BLOCK

    @pl.when(j == 0)
    def _():
        m_ref[...] = jnp.full(m_ref.shape, NEG, F32)
        l_ref[...] = jnp.zeros(l_ref.shape, F32)
        acc_ref[...] = jnp.zeros(acc_ref.shape, F32)

    @pl.when(j * tk <= i * tq + tq - 1)
    def _():
        q4 = q_ref[...].reshape(r_heads * tq, HEAD_DIM)
        s = _dot_nt(q4, k_ref[0])
        b_idx = lax.broadcasted_iota(jnp.int32, (n_slc, tk), 0)
        k_idx = lax.broadcasted_iota(jnp.int32, (n_slc, tk), 1)
        expand = jnp.where(b_idx == j * blocks_per_tile + (k_idx >> SEL_SHIFT), 1.0, 0.0).astype(MXU_DTYPE)
        picked = _dot(sel_ref[0], expand)
        tok = i * tq + lax.broadcasted_iota(jnp.int32, (tq, 1), 0)
        kpos = j * tk + lax.broadcasted_iota(jnp.int32, (1, tk), 1)
        mask = (picked > 0.5) & (kpos <= tok)
        p_rows = []
        for r in range(r_heads):
            rows = slice(r * tq, (r + 1) * tq)
            s_r = s[rows, :]
            m_prev = m_ref[rows, :]
            m_new = jnp.maximum(m_prev, jnp.max(jnp.where(mask, s_r, NEG), axis=-1, keepdims=True))
            p = jnp.where(mask, jnp.exp(SCALE * (s_r - m_new)), 0.0)
            alpha = jnp.exp(SCALE * (m_prev - m_new))
            l_ref[rows, :] = alpha * l_ref[rows, :] + jnp.sum(p, axis=-1, keepdims=True)
            acc_ref[rows, :] = alpha * acc_ref[rows, :]
            m_ref[rows, :] = m_new
            p_rows.append(p.astype(MXU_DTYPE))
        acc_ref[...] += _dot(jnp.concatenate(p_rows, axis=0), v_ref[0])

    @pl.when(j == pl.num_programs(2) - 1)
    def _():
        gate = gate_ref[...]
        for r in range(r_heads):
            rows = slice(r * tq, (r + 1) * tq)
            cols = slice(r * HEAD_DIM, (r + 1) * HEAD_DIM)
            o_s = acc_ref[rows, :] / l_ref[rows, :]
            o_ref[:, cols] = (ocw_ref[:, cols] + gate[:, 3 * r + 1:3 * r + 2] * o_s).astype(o_ref.dtype)


def _sel_attention(qkv, sel, ocw, gates, s):
    tq = min(s, 256)
    tk = min(s, 512)
    g = N_KV_GROUPS
    n_slc = s // SEL_BLOCK
    last_tile = lambda i, j: jnp.minimum(j, (i * tq + tq - 1) // tk)
    return pl.pallas_call(
        functools.partial(_sel_kernel, tq=tq, tk=tk),
        grid=(g, s // tq, s // tk),
        in_specs=[
            pl.BlockSpec((HEADS_PER_GROUP, tq, HEAD_DIM), lambda a, i, j: (a, i, 0)),
            pl.BlockSpec((1, tk, HEAD_DIM), lambda a, i, j: (N_HEADS + a, last_tile(i, j), 0)),
            pl.BlockSpec((1, tk, HEAD_DIM), lambda a, i, j: (N_HEADS + g + a, last_tile(i, j), 0)),
            pl.BlockSpec((1, tq, n_slc), lambda a, i, j: (a, i, 0)),
            pl.BlockSpec((tq, HEADS_PER_GROUP * HEAD_DIM), lambda a, i, j: (i, a)),
            pl.BlockSpec((tq, HEAD_DIM), lambda a, i, j: (i, a)),
        ],
        out_specs=pl.BlockSpec((tq, HEADS_PER_GROUP * HEAD_DIM), lambda a, i, j: (i, a)),
        out_shape=jax.ShapeDtypeStruct((s, N_HEADS * HEAD_DIM), MXU_DTYPE),
        scratch_shapes=[
            pltpu.VMEM((HEADS_PER_GROUP * tq, 1), F32),
            pltpu.VMEM((HEADS_PER_GROUP * tq, 1), F32),
            pltpu.VMEM((HEADS_PER_GROUP * tq, HEAD_DIM), F32),
        ],
        compiler_params=_params("parallel", "parallel", "arbitrary"),
        name="sel_attention",
    )(qkv, qkv, qkv, sel, ocw, gates)


def _merge_kernel(h_ref, cf_ref, at_ref, wga_ref, wgb_ref, wco_ref, wno_ref, o_ref):
    h = h_ref[...]
    y_a = _dot(cf_ref[...], wco_ref[...])
    y_b = _dot(at_ref[...], wno_ref[...])
    merged = jax.nn.sigmoid(_dot(h, wga_ref[...])) * y_a + jax.nn.sigmoid(_dot(h, wgb_ref[...])) * y_b
    o_ref[...] = merged.astype(o_ref.dtype)


def _merge(h, cf, attn, w_ga, w_gb, w_co, w_no):
    s, d = h.shape
    tm = min(s, 1024)
    tn = 512
    row = lambda width: pl.BlockSpec((tm, width), lambda i, j: (i, 0))
    col = lambda depth: pl.BlockSpec((depth, tn), lambda i, j: (0, j))
    return pl.pallas_call(
        _merge_kernel,
        grid=(s // tm, d // tn),
        in_specs=[row(d), row(CONV_WIDTH), row(N_HEADS * HEAD_DIM), col(d), col(d), col(CONV_WIDTH), col(N_HEADS * HEAD_DIM)],
        out_specs=pl.BlockSpec((tm, tn), lambda i, j: (i, j)),
        out_shape=jax.ShapeDtypeStruct((s, d), MXU_DTYPE),
        compiler_params=_params("parallel", "parallel"),
        name="merge",
    )(h, cf, attn, w_ga, w_gb, w_co, w_no)


def _out_proj_kernel(m_ref, w_ref, x_ref, gain_ref, g_ref, o_ref):
    y = _dot(m_ref[...], w_ref[...])
    o_ref[...] = x_ref[...] + g_ref[...] * _rms(y, gain_ref[...])


def _out_proj(merged, w_out, x, gain, g1):
    s, d = x.shape
    tm = min(s, 512)
    vec = pl.BlockSpec((1, d), lambda i: (0, 0))
    row = pl.BlockSpec((tm, d), lambda i: (i, 0))
    return pl.pallas_call(
        _out_proj_kernel,
        grid=(s // tm,),
        in_specs=[row, pl.BlockSpec((d, d), lambda i: (0, 0)), row, vec, vec],
        out_specs=row,
        out_shape=jax.ShapeDtypeStruct((s, d), F32),
        compiler_params=_params("parallel"),
        name="out_proj",
    )(merged, w_out, x, gain, g1)


def _mlp_kernel(x_ref, gin_ref, sc_ref, sh_ref, wu_ref, wd_ref, gout_ref, g_ref, o_ref, h_ref, acc_ref):
    f = pl.program_id(1)

    @pl.when(f == 0)
    def _():
        y = _rms(x_ref[...], gin_ref[...])
        h_ref[...] = (y * (1.0 + sc_ref[...]) + sh_ref[...]).astype(h_ref.dtype)
        acc_ref[...] = jnp.zeros(acc_ref.shape, F32)

    u = jnp.square(jnp.maximum(_dot(h_ref[...], wu_ref[...]), 0.0))
    acc_ref[...] += _dot(u.astype(MXU_DTYPE), wd_ref[...])

    @pl.when(f == pl.num_programs(1) - 1)
    def _():
        o_ref[...] = x_ref[...] + g_ref[...] * _rms(acc_ref[...], gout_ref[...])


def _mlp(x, gain_in, sc, sh, w_up, w_down, gain_out, g2):
    s, d = x.shape
    tm = min(s, 512)
    tf = 1024
    vec = pl.BlockSpec((1, d), lambda i, f: (0, 0))
    row = pl.BlockSpec((tm, d), lambda i, f: (i, 0))
    return pl.pallas_call(
        _mlp_kernel,
        grid=(s // tm, D_FF // tf),
        in_specs=[row, vec, vec, vec, pl.BlockSpec((d, tf), lambda i, f: (0, f)), pl.BlockSpec((tf, d), lambda i, f: (f, 0)), vec, vec],
        out_specs=row,
        out_shape=jax.ShapeDtypeStruct((s, d), F32),
        scratch_shapes=[pltpu.VMEM((tm, d), MXU_DTYPE), pltpu.VMEM((tm, d), F32)],
        compiler_params=_params("parallel", "arbitrary"),
        name="mlp",
    )(x, gain_in, sc, sh, w_up, w_down, gain_out, g2)


def _gate_weight(w_in_l):
    w = w_in_l[:, COL_GATE:COL_GA].reshape(D_MODEL, N_KV_GROUPS, GATES_PER_GROUP)
    w = jnp.pad(w, ((0, 0), (0, 0), (0, HEAD_DIM - GATES_PER_GROUP)))
    return w.reshape(D_MODEL, N_KV_GROUPS * HEAD_DIM).astype(MXU_DTYPE)


def kernel(x, c, positions, ada_w, ada_b, norm_gains, w_in, conv_w, w_conv_out, cmp_pe, cmp_w1, cmp_b1, cmp_w2, cmp_b2,
           w_nsa_out, w_out, w_mlp_up, w_mlp_down):
    b, s, d = x.shape
    assert b == 1 and d == D_MODEL
    depth = ada_w.shape[0]
    cast = lambda w: w.astype(MXU_DTYPE)
    xs = x.reshape(s, d)
    mod = _adaln(c, ada_w, ada_b)
    tables = _rope_tables(positions)
    ks_col = COL_KV + 2 * KV_WIDTH
    for l in range(depth):
        sh1, sc1, g1, sh2, sc2, g2 = [mod[l, k * d:(k + 1) * d].reshape(1, d) for k in range(6)]
        gains = [norm_gains[l, k].reshape(1, d) for k in range(4)]
        wl = w_in[l]
        h = _norm_mod(xs, gains[0], sc1, sh1)
        cf = _conv_proj(h, cast(wl[:, :COL_Q]), conv_w[l])
        w_qkv = cast(jnp.concatenate([wl[:, COL_Q:COL_KV], wl[:, ks_col:COL_GATE]], axis=1))
        qkv = _head_proj(h, w_qkv, tables, (0, 1, 2, 3, 4, 6), MXU_DTYPE)
        kv_raw = _head_proj(h, cast(wl[:, COL_KV:ks_col]), tables, (0,), F32)
        gates = _gate_proj(h, _gate_weight(wl))
        kvc = _compress(kv_raw, cmp_pe[l], cast(cmp_w1[l]), cmp_b1[l], cast(cmp_w2[l]), cmp_b2[l])
        ocw, sel = _cmp_win(qkv, kvc, gates, s)
        attn = _sel_attention(qkv, sel, ocw, gates, s)
        merged = _merge(h, cf, attn, cast(wl[:, COL_GA:COL_GB]), cast(wl[:, COL_GB:]), cast(w_conv_out[l]), cast(w_nsa_out[l]))
        xs = _out_proj(merged, cast(w_out[l]), xs, gains[1], g1)
        xs = _mlp(xs, gains[2], sc2, sh2, cast(w_mlp_up[l]), cast(w_mlp_down[l]), gains[3], g2)
    return xs.reshape(b, s, d)
```

```python
import functools

import jax
import jax.numpy as jnp
from jax import lax
from jax.experimental import pallas as pl
from jax.experimental.pallas import tpu as pltpu

F32 = jnp.float32
MXU_DTYPE = jnp.bfloat16

D_MODEL = 2048
CONV_WIDTH = D_MODEL // 2
N_HEADS = 16
N_KV_GROUPS = 4
HEADS_PER_GROUP = N_HEADS // N_KV_GROUPS
HEAD_DIM = D_MODEL // N_HEADS
KV_WIDTH = N_KV_GROUPS * HEAD_DIM
ROPE_DIM = HEAD_DIM // 4
ROPE_THETA = 500000.0
CMP_BLOCK = 32
CMP_STRIDE = 16
CMP_HIDDEN = 2 * HEAD_DIM
SEL_BLOCK = 64
SEL_SHIFT = 6
SEL_TOP = 16
WINDOW = 512
D_FF = 4 * D_MODEL
NORM_EPS = 1e-6
NEG = -1e30
FORCE = 1e30
REMOVED = -3e38
SCALE = HEAD_DIM ** -0.5
EXP2_SCALE = SCALE * 1.4426950408889634

COL_Q = 3 * CONV_WIDTH
COL_KV = COL_Q + N_HEADS * HEAD_DIM
COL_GATE = COL_KV + 6 * KV_WIDTH
COL_GA = COL_GATE + 3 * N_HEADS
COL_GB = COL_GA + D_MODEL
GATES_PER_GROUP = 3 * HEADS_PER_GROUP

VMEM_LIMIT_BYTES = 56 * 1024 * 1024


def _params(*semantics):
    return pltpu.CompilerParams(dimension_semantics=semantics, vmem_limit_bytes=VMEM_LIMIT_BYTES)


def _dot(a, b):
    return jnp.dot(a, b, preferred_element_type=F32)


def _dot_nt(a, b):
    return lax.dot_general(a, b, (((1,), (1,)), ((), ())), preferred_element_type=F32)


def _rms(y, gain):
    ms = jnp.mean(y * y, axis=-1, keepdims=True)
    return y * lax.rsqrt(ms + NORM_EPS) * gain


def _adaln_kernel(c_ref, w_ref, b_ref, o_ref):
    c = c_ref[...]
    act = (c * jax.nn.sigmoid(c)).astype(MXU_DTYPE)
    o_ref[0] = _dot(act, w_ref[0].astype(MXU_DTYPE)) + b_ref[0]


def _adaln(c, ada_w, ada_b):
    depth, d, n = ada_w.shape
    tn = 1024
    c8 = jnp.broadcast_to(c, (8, d))
    out = pl.pallas_call(
        _adaln_kernel,
        grid=(depth, n // tn),
        in_specs=[
            pl.BlockSpec((8, d), lambda l, j: (0, 0)),
            pl.BlockSpec((1, d, tn), lambda l, j: (l, 0, j)),
            pl.BlockSpec((1, 1, tn), lambda l, j: (l, 0, j)),
        ],
        out_specs=pl.BlockSpec((1, 8, tn), lambda l, j: (l, 0, j)),
        out_shape=jax.ShapeDtypeStruct((depth, 8, n), F32),
        compiler_params=_params("parallel", "parallel"),
        name="adaln",
    )(c8, ada_w, ada_b.reshape(depth, 1, n))
    return out[:, 0, :]


def _rope_table_kernel(pos_ref, freq_ref, c_ref, s1_ref, s2_ref):
    half = ROPE_DIM // 2
    ang = pos_ref[...].astype(F32) * freq_ref[...]
    lane = lax.broadcasted_iota(jnp.int32, ang.shape, 1)
    cos = jnp.cos(ang)
    sin = jnp.sin(ang)
    c_ref[...] = jnp.where(lane < ROPE_DIM, cos, 1.0)
    s1_ref[...] = jnp.where(lane < half, -sin, 0.0)
    s2_ref[...] = jnp.where((lane >= half) & (lane < ROPE_DIM), sin, 0.0)


def _rope_tables(positions):
    s = positions.shape[-1]
    tm = min(s, 2048)
    half = ROPE_DIM // 2
    inv_freq = ROPE_THETA ** (-jnp.arange(0, ROPE_DIM, 2, dtype=F32) / ROPE_DIM)
    freq = jnp.concatenate([inv_freq, inv_freq, jnp.zeros((HEAD_DIM - 2 * half,), F32)]).reshape(1, HEAD_DIM)
    spec = pl.BlockSpec((tm, HEAD_DIM), lambda i: (i, 0))
    shape = jax.ShapeDtypeStruct((s, HEAD_DIM), F32)
    return pl.pallas_call(
        _rope_table_kernel,
        grid=(s // tm,),
        in_specs=[pl.BlockSpec((tm, 1), lambda i: (i, 0)), pl.BlockSpec((1, HEAD_DIM), lambda i: (0, 0))],
        out_specs=[spec, spec, spec],
        out_shape=[shape, shape, shape],
        compiler_params=_params("parallel"),
        name="rope_tables",
    )(positions.reshape(s, 1), freq)


def _rope(x, c, s1, s2):
    half = ROPE_DIM // 2
    return x * c + pltpu.roll(x, HEAD_DIM - half, axis=1) * s1 + pltpu.roll(x, half, axis=1) * s2


def _norm_mod_kernel(x_ref, g_ref, sc_ref, sh_ref, o_ref):
    y = _rms(x_ref[...], g_ref[...])
    o_ref[...] = (y * (1.0 + sc_ref[...]) + sh_ref[...]).astype(o_ref.dtype)


def _norm_mod(x, gain, sc, sh):
    s, d = x.shape
    tm = min(s, 512)
    vec = pl.BlockSpec((1, d), lambda i: (0, 0))
    return pl.pallas_call(
        _norm_mod_kernel,
        grid=(s // tm,),
        in_specs=[pl.BlockSpec((tm, d), lambda i: (i, 0)), vec, vec, vec],
        out_specs=pl.BlockSpec((tm, d), lambda i: (i, 0)),
        out_shape=jax.ShapeDtypeStruct((s, d), MXU_DTYPE),
        compiler_params=_params("parallel"),
        name="norm_mod",
    )(x, gain, sc, sh)


def _conv_proj_kernel(h_ref, wb_ref, wc_ref, wx_ref, cw_ref, o_ref, carry_ref):
    i = pl.program_id(0)
    j = pl.program_id(1)
    tm = h_ref.shape[0]
    h = h_ref[...]
    u = _dot(h, wc_ref[...]) * _dot(h, wx_ref[...])

    @pl.when(i == 0)
    def _():
        carry_ref[j] = jnp.zeros(carry_ref.shape[1:], F32)

    prev = carry_ref[j]
    carry_ref[j] = u[tm - 8:, :]
    row = lax.broadcasted_iota(jnp.int32, u.shape, 0)
    u1 = jnp.where(row == 0, prev[7:8, :], pltpu.roll(u, 1, axis=0))
    u2 = jnp.where(row == 0, prev[6:7, :], jnp.where(row == 1, prev[7:8, :], pltpu.roll(u, 2, axis=0)))
    cw = cw_ref[...]
    z = cw[2:3, :] * u + cw[1:2, :] * u1 + cw[0:1, :] * u2
    o_ref[...] = (_dot(h, wb_ref[...]) * z).astype(o_ref.dtype)


def _conv_proj(h, w_conv, conv_w):
    s, d = h.shape
    tm = min(s, 1024)
    tn = 256
    nb = CONV_WIDTH // tn
    return pl.pallas_call(
        _conv_proj_kernel,
        grid=(s // tm, nb),
        in_specs=[
            pl.BlockSpec((tm, d), lambda i, j: (i, 0)),
            pl.BlockSpec((d, tn), lambda i, j: (0, j)),
            pl.BlockSpec((d, tn), lambda i, j: (0, j + nb)),
            pl.BlockSpec((d, tn), lambda i, j: (0, j + 2 * nb)),
            pl.BlockSpec((3, tn), lambda i, j: (0, j)),
        ],
        out_specs=pl.BlockSpec((tm, tn), lambda i, j: (i, j)),
        out_shape=jax.ShapeDtypeStruct((s, CONV_WIDTH), MXU_DTYPE),
        scratch_shapes=[pltpu.VMEM((nb, 8, tn), F32)],
        compiler_params=_params("arbitrary", "arbitrary"),
        name="conv_proj",
    )(h, w_conv, w_conv, w_conv, conv_w)


def _head_proj_kernel(h_ref, w_ref, c_ref, s1_ref, s2_ref, o_ref, *, rope_groups):
    j = pl.program_id(1)
    acc = _dot(h_ref[...], w_ref[...])
    heads = [acc[:, r * HEAD_DIM:(r + 1) * HEAD_DIM] for r in range(HEADS_PER_GROUP)]
    is_rope = functools.reduce(jnp.logical_or, [j == r for r in rope_groups])

    @pl.when(is_rope)
    def _():
        c, s1, s2 = c_ref[...], s1_ref[...], s2_ref[...]
        for r, x in enumerate(heads):
            o_ref[r] = _rope(x, c, s1, s2).astype(o_ref.dtype)

    @pl.when(jnp.logical_not(is_rope))
    def _():
        for r, x in enumerate(heads):
            o_ref[r] = x.astype(o_ref.dtype)


def _head_proj(h, w, tables, rope_groups, out_dtype):
    s, d = h.shape
    tm = min(s, 1024)
    tn = HEADS_PER_GROUP * HEAD_DIM
    ng = w.shape[1] // tn
    tab = pl.BlockSpec((tm, HEAD_DIM), lambda i, j: (i, 0))
    return pl.pallas_call(
        functools.partial(_head_proj_kernel, rope_groups=rope_groups),
        grid=(s // tm, ng),
        in_specs=[pl.BlockSpec((tm, d), lambda i, j: (i, 0)), pl.BlockSpec((d, tn), lambda i, j: (0, j)), tab, tab, tab],
        out_specs=pl.BlockSpec((HEADS_PER_GROUP, tm, HEAD_DIM), lambda i, j: (j, i, 0)),
        out_shape=jax.ShapeDtypeStruct((ng * HEADS_PER_GROUP, s, HEAD_DIM), out_dtype),
        compiler_params=_params("parallel", "parallel"),
        name="head_proj",
    )(h, w, *tables)


def _gate_proj_kernel(h_ref, w_ref, o_ref):
    o_ref[...] = jax.nn.sigmoid(_dot(h_ref[...], w_ref[...]))


def _gate_proj(h, w_gate):
    s, d = h.shape
    tm = min(s, 1024)
    n = w_gate.shape[1]
    return pl.pallas_call(
        _gate_proj_kernel,
        grid=(s // tm,),
        in_specs=[pl.BlockSpec((tm, d), lambda i: (i, 0)), pl.BlockSpec((d, n), lambda i: (0, 0))],
        out_specs=pl.BlockSpec((tm, n), lambda i: (i, 0)),
        out_shape=jax.ShapeDtypeStruct((s, n), F32),
        compiler_params=_params("parallel"),
        name="gate_proj",
    )(h, w_gate)


def _compress_kernel(x_ref, pe_ref, w1_ref, b1_ref, w2_ref, b2_ref, o_ref):
    n = o_ref.shape[1]
    k_lo = CMP_STRIDE * HEAD_DIM
    pe = pe_ref[0]
    lo, hi = [], []
    for l in range(CMP_STRIDE):
        rows = x_ref[0, pl.ds(l, n, stride=CMP_STRIDE), :]
        lo.append((rows + pe[l:l + 1, :]).astype(MXU_DTYPE))
        hi.append((rows + pe[CMP_STRIDE + l:CMP_STRIDE + l + 1, :]).astype(MXU_DTYPE))
    t_lo = _dot(jnp.concatenate(lo, axis=1), w1_ref[0, :k_lo, :])
    t_hi = _dot(jnp.concatenate(hi, axis=1), w1_ref[0, k_lo:, :])
    hid = t_lo + pltpu.roll(t_hi, n - 1, axis=0) + b1_ref[0]
    out = _dot(jax.nn.gelu(hid).astype(MXU_DTYPE), w2_ref[0]) + b2_ref[0]
    row = lax.broadcasted_iota(jnp.int32, out.shape, 0)
    o_ref[0] = jnp.where(row == n - 1, 0.0, out).astype(o_ref.dtype)


def _compress(kv_raw, pe, w1, b1, w2, b2):
    nh, s, dh = kv_raw.shape
    n = s // CMP_STRIDE
    g = N_KV_GROUPS
    return pl.pallas_call(
        _compress_kernel,
        grid=(nh,),
        in_specs=[
            pl.BlockSpec((1, s, dh), lambda a: (a, 0, 0)),
            pl.BlockSpec((1, CMP_BLOCK, dh), lambda a: (a // g, 0, 0)),
            pl.BlockSpec((1, CMP_BLOCK * dh, CMP_HIDDEN), lambda a: (a // g, 0, 0)),
            pl.BlockSpec((1, 1, CMP_HIDDEN), lambda a: (a // g, 0, 0)),
            pl.BlockSpec((1, CMP_HIDDEN, dh), lambda a: (a // g, 0, 0)),
            pl.BlockSpec((1, 1, dh), lambda a: (a // g, 0, 0)),
        ],
        out_specs=pl.BlockSpec((1, n, dh), lambda a: (a, 0, 0)),
        out_shape=jax.ShapeDtypeStruct((nh, n, dh), MXU_DTYPE),
        compiler_params=_params("parallel"),
        name="compress",
    )(kv_raw, pe, w1, b1.reshape(2, 1, CMP_HIDDEN), w2, b2.reshape(2, 1, dh))


def _softmax_numerators(s, bias):
    sm = s + bias
    m = jnp.max(sm, axis=-1, keepdims=True)
    m = jnp.where(m < 0.5 * NEG, 0.0, m)
    return jnp.exp2(EXP2_SCALE * (sm - m)).astype(MXU_DTYPE)


def _cmp_win_kernel(q_ref, kc_ref, vc_ref, kw2_ref, kw1_ref, kw0_ref, vw2_ref, vw1_ref, vw0_ref, ov_ref, gate_ref,
                    ocw_ref, sel_ref, work_ref, *, tq):
    i = pl.program_id(1)
    r_heads = HEADS_PER_GROUP
    s0 = i * tq
    q4 = q_ref[...].reshape(r_heads * tq, HEAD_DIM)
    tok = s0 + lax.broadcasted_iota(jnp.int32, (tq, 1), 0)

    kc = kc_ref[0]
    n_cmp = kc.shape[0]
    s_c = _dot_nt(q4, kc)
    cmp_end = lax.broadcasted_iota(jnp.int32, (1, n_cmp), 1) * CMP_STRIDE + (CMP_BLOCK - 1)
    bias_c = jnp.where(cmp_end <= tok, 0.0, NEG)
    p_c = jnp.concatenate([_softmax_numerators(s_c[r * tq:(r + 1) * tq, :], bias_c) for r in range(r_heads)], axis=0)
    ones_c = jnp.ones((n_cmp, HEAD_DIM), MXU_DTYPE)
    res_c = _dot(p_c, jnp.concatenate([vc_ref[0], ones_c, ov_ref[...]], axis=1))
    o_c, imp = [], None
    for r in range(r_heads):
        rows = slice(r * tq, (r + 1) * tq)
        l = res_c[rows, HEAD_DIM:2 * HEAD_DIM]
        inv = jnp.where(l > 0.0, 1.0 / l, 0.0)
        o_c.append(res_c[rows, :HEAD_DIM] * inv)
        part = res_c[rows, 2 * HEAD_DIM:] * inv[:, :1]
        imp = part if imp is None else imp + part
    n_slc = imp.shape[1]
    blk = lax.broadcasted_iota(jnp.int32, (1, n_slc), 1)
    cur = tok >> SEL_SHIFT
    forced = (blk == cur) | (blk == 0)
    imp = jnp.where(forced, FORCE, jnp.where(blk <= cur, imp, NEG))

    blk_f = blk.astype(F32)
    work_ref[...] = imp
    for _ in range(min(SEL_TOP, n_slc)):
        w = work_ref[...]
        best = jnp.max(w, axis=-1, keepdims=True)
        idx = jnp.min(jnp.where(w == best, blk_f, float(n_slc)), axis=-1, keepdims=True)
        work_ref[...] = jnp.where(blk_f == idx, REMOVED, w)
    sel_ref[0] = jnp.where(work_ref[...] == REMOVED, 0.0, NEG).astype(sel_ref.dtype)

    kwin = jnp.concatenate([kw2_ref[0], kw1_ref[0], kw0_ref[0]], axis=0)
    vwin = jnp.concatenate([vw2_ref[0], vw1_ref[0], vw0_ref[0]], axis=0)
    s_w = _dot_nt(q4, kwin)
    wpos = s0 - 2 * tq + lax.broadcasted_iota(jnp.int32, (1, 3 * tq), 1)
    diff = tok - wpos
    bias_w = jnp.where((diff >= 0) & (diff < WINDOW) & (wpos >= 0), 0.0, NEG)
    p_w = jnp.concatenate([_softmax_numerators(s_w[r * tq:(r + 1) * tq, :], bias_w) for r in range(r_heads)], axis=0)
    res_w = _dot(p_w, jnp.concatenate([vwin, jnp.ones((3 * tq, HEAD_DIM), MXU_DTYPE)], axis=1))

    gate = gate_ref[...]
    for r in range(r_heads):
        rows = slice(r * tq, (r + 1) * tq)
        g_cmp = gate[:, 3 * r:3 * r + 1]
        g_win = gate[:, 3 * r + 2:3 * r + 3]
        o_w = res_w[rows, :HEAD_DIM] / res_w[rows, HEAD_DIM:]
        ocw_ref[:, r * HEAD_DIM:(r + 1) * HEAD_DIM] = g_cmp * o_c[r] + g_win * o_w


def _overlap_matrix(n_cmp_rows, n_slc):
    cs = jnp.arange(n_cmp_rows)[:, None] * CMP_STRIDE
    ss = jnp.arange(n_slc)[None, :] * SEL_BLOCK
    ov = jnp.clip(jnp.minimum(cs + CMP_BLOCK, ss + SEL_BLOCK) - jnp.maximum(cs, ss), 0, None)
    return (ov.astype(F32) / CMP_BLOCK).astype(MXU_DTYPE)


def _cmp_win(qkv, kvc, gates, s):
    tq = min(s, 256)
    assert WINDOW <= 2 * tq
    g = N_KV_GROUPS
    n_cmp = kvc.shape[1]
    n_slc = s // SEL_BLOCK
    head = lambda base: [
        pl.BlockSpec((1, tq, HEAD_DIM), functools.partial(lambda a, i, d, base: (base + a, jnp.maximum(i - d, 0), 0), d=d, base=base))
        for d in (2, 1, 0)
    ]
    ocw, sel = pl.pallas_call(
        functools.partial(_cmp_win_kernel, tq=tq),
        grid=(g, s // tq),
        in_specs=[
            pl.BlockSpec((HEADS_PER_GROUP, tq, HEAD_DIM), lambda a, i: (a, i, 0)),
            pl.BlockSpec((1, n_cmp, HEAD_DIM), lambda a, i: (a, 0, 0)),
            pl.BlockSpec((1, n_cmp, HEAD_DIM), lambda a, i: (g + a, 0, 0)),
            *head(N_HEADS + 2 * g),
            *head(N_HEADS + 3 * g),
            pl.BlockSpec((n_cmp, n_slc), lambda a, i: (0, 0)),
            pl.BlockSpec((tq, HEAD_DIM), lambda a, i: (i, a)),
        ],
        out_specs=[
            pl.BlockSpec((tq, HEADS_PER_GROUP * HEAD_DIM), lambda a, i: (i, a)),
            pl.BlockSpec((1, tq, n_slc), lambda a, i: (a, i, 0)),
        ],
        out_shape=[
            jax.ShapeDtypeStruct((s, N_HEADS * HEAD_DIM), F32),
            jax.ShapeDtypeStruct((g, s, n_slc), MXU_DTYPE),
        ],
        scratch_shapes=[pltpu.VMEM((tq, n_slc), F32)],
        compiler_params=_params("parallel", "parallel"),
        name="cmp_win",
    )(qkv, kvc, kvc, qkv, qkv, qkv, qkv, qkv, qkv, _overlap_matrix(n_cmp, n_slc), gates)
    return ocw, sel


def _sel_kernel(q_ref, k_ref, v_ref, sb_ref, ocw_ref, gate_ref, o_ref, s_ref, m_ref, acc_ref, *, tq, tk):
    i = pl.program_id(1)
    r_heads = HEADS_PER_GROUP
    n_slc = sb_ref.shape[2]
    blocks_per_tile = tk // SEL_BLOCK
    lane_tiles = tk // HEAD_DIM
    n_tiles = (i * tq + tq - 1) // tk + 1
    q4 = q_ref[...].reshape(r_heads * tq, HEAD_DIM)
    tok = i * tq + lax.broadcasted_iota(jnp.int32, (tq, 1), 0)
    b_idx = lax.broadcasted_iota(jnp.int32, (n_slc, tk), 0)
    key_blk = lax.broadcasted_iota(jnp.int32, (n_slc, tk), 1) >> SEL_SHIFT
    key_off = lax.broadcasted_iota(jnp.int32, (1, tk), 1)

    def produce(slot, j):
        start = pl.multiple_of(j * tk, tk)
        s = _dot_nt(q4, k_ref[0, pl.ds(start, tk), :])
        expand = jnp.where(b_idx == j * blocks_per_tile + key_blk, 1.0, 0.0).astype(MXU_DTYPE)
        bias = jnp.where(j * tk + key_off <= tok, _dot(sb_ref[0], expand), NEG)
        for r in range(r_heads):
            rows = slice(r * tq, (r + 1) * tq)
            s_ref[slot, rows, :] = s[rows, :] + bias

    def consume(slot, j):
        start = pl.multiple_of(j * tk, tk)
        v_ext = jnp.concatenate([v_ref[0, pl.ds(start, tk), :], jnp.ones((tk, HEAD_DIM), MXU_DTYPE)], axis=1)
        p_rows, alphas = [], []
        for r in range(r_heads):
            rows = slice(r * tq, (r + 1) * tq)
            sm = s_ref[slot, rows, :]
            m_prev = m_ref[rows, :]
            m_new = jnp.maximum(m_prev, jnp.max(sm, axis=-1, keepdims=True))
            m_ref[rows, :] = m_new
            p_rows.append(jnp.exp2(EXP2_SCALE * (sm - pltpu.repeat(m_new, lane_tiles, axis=1))).astype(MXU_DTYPE))
            alphas.append(jnp.exp2(EXP2_SCALE * (m_prev - m_new)))
        pv = _dot(jnp.concatenate(p_rows, axis=0), v_ext)
        for r in range(r_heads):
            rows = slice(r * tq, (r + 1) * tq)
            acc_ref[rows, :] = acc_ref[rows, :] * pltpu.repeat(alphas[r], 2, axis=1) + pv[rows, :]

    m_ref[...] = jnp.full(m_ref.shape, NEG, F32)
    acc_ref[...] = jnp.zeros(acc_ref.shape, F32)
    produce(0, 0)

    def body(j, carry):
        slot = j & 1
        consume(slot, j)
        produce(1 - slot, j + 1)
        return carry

    lax.fori_loop(0, n_tiles - 1, body, 0)
    consume((n_tiles - 1) & 1, n_tiles - 1)

    gate = gate_ref[...]
    for r in range(r_heads):
        rows = slice(r * tq, (r + 1) * tq)
        cols = slice(r * HEAD_DIM, (r + 1) * HEAD_DIM)
        o_s = acc_ref[rows, :HEAD_DIM] / acc_ref[rows, HEAD_DIM:]
        o_ref[:, cols] = (ocw_ref[:, cols] + gate[:, 3 * r + 1:3 * r + 2] * o_s).astype(o_ref.dtype)


def _sel_attention(qkv, sel_bias, ocw, gates, s):
    tq = min(s, 256)
    tk = min(s, 512)
    g = N_KV_GROUPS
    n_slc = s // SEL_BLOCK
    width = HEADS_PER_GROUP * HEAD_DIM
    rows = HEADS_PER_GROUP * tq
    return pl.pallas_call(
        functools.partial(_sel_kernel, tq=tq, tk=tk),
        grid=(g, s // tq),
        in_specs=[
            pl.BlockSpec((HEADS_PER_GROUP, tq, HEAD_DIM), lambda a, i: (a, i, 0)),
            pl.BlockSpec((1, s, HEAD_DIM), lambda a, i: (N_HEADS + a, 0, 0)),
            pl.BlockSpec((1, s, HEAD_DIM), lambda a, i: (N_HEADS + g + a, 0, 0)),
            pl.BlockSpec((1, tq, n_slc), lambda a, i: (a, i, 0)),
            pl.BlockSpec((tq, width), lambda a, i: (i, a)),
            pl.BlockSpec((tq, HEAD_DIM), lambda a, i: (i, a)),
        ],
        out_specs=pl.BlockSpec((tq, width), lambda a, i: (i, a)),
        out_shape=jax.ShapeDtypeStruct((s, N_HEADS * HEAD_DIM), MXU_DTYPE),
        scratch_shapes=[
            pltpu.VMEM((2, rows, tk), F32),
            pltpu.VMEM((rows, HEAD_DIM), F32),
            pltpu.VMEM((rows, 2 * HEAD_DIM), F32),
        ],
        compiler_params=_params("parallel", "arbitrary"),
        name="sel_attention",
    )(qkv, qkv, qkv, sel_bias, ocw, gates)


def _merge_kernel(h_ref, cf_ref, at_ref, wga_ref, wgb_ref, wco_ref, wno_ref, o_ref):
    h = h_ref[...]
    y_a = _dot(cf_ref[...], wco_ref[...])
    y_b = _dot(at_ref[...], wno_ref[...])
    merged = jax.nn.sigmoid(_dot(h, wga_ref[...])) * y_a + jax.nn.sigmoid(_dot(h, wgb_ref[...])) * y_b
    o_ref[...] = merged.astype(o_ref.dtype)


def _merge(h, cf, attn, w_ga, w_gb, w_co, w_no):
    s, d = h.shape
    tm = min(s, 1024)
    tn = 512
    row = lambda width: pl.BlockSpec((tm, width), lambda i, j: (i, 0))
    col = lambda depth: pl.BlockSpec((depth, tn), lambda i, j: (0, j))
    return pl.pallas_call(
        _merge_kernel,
        grid=(s // tm, d // tn),
        in_specs=[row(d), row(CONV_WIDTH), row(N_HEADS * HEAD_DIM), col(d), col(d), col(CONV_WIDTH), col(N_HEADS * HEAD_DIM)],
        out_specs=pl.BlockSpec((tm, tn), lambda i, j: (i, j)),
        out_shape=jax.ShapeDtypeStruct((s, d), MXU_DTYPE),
        compiler_params=_params("parallel", "parallel"),
        name="merge",
    )(h, cf, attn, w_ga, w_gb, w_co, w_no)


def _out_proj_kernel(m_ref, w_ref, x_ref, gain_ref, g_ref, o_ref):
    y = _dot(m_ref[...], w_ref[...])
    o_ref[...] = x_ref[...] + g_ref[...] * _rms(y, gain_ref[...])


def _out_proj(merged, w_out, x, gain, g1):
    s, d = x.shape
    tm = min(s, 512)
    vec = pl.BlockSpec((1, d), lambda i: (0, 0))
    row = pl.BlockSpec((tm, d), lambda i: (i, 0))
    return pl.pallas_call(
        _out_proj_kernel,
        grid=(s // tm,),
        in_specs=[row, pl.BlockSpec((d, d), lambda i: (0, 0)), row, vec, vec],
        out_specs=row,
        out_shape=jax.ShapeDtypeStruct((s, d), F32),
        compiler_params=_params("parallel"),
        name="out_proj",
    )(merged, w_out, x, gain, g1)


def _mlp_kernel(x_ref, gin_ref, sc_ref, sh_ref, wu_ref, wd_ref, gout_ref, g_ref, o_ref, h_ref, acc_ref):
    f = pl.program_id(1)

    @pl.when(f == 0)
    def _():
        y = _rms(x_ref[...], gin_ref[...])
        h_ref[...] = (y * (1.0 + sc_ref[...]) + sh_ref[...]).astype(h_ref.dtype)
        acc_ref[...] = jnp.zeros(acc_ref.shape, F32)

    u = jnp.square(jnp.maximum(_dot(h_ref[...], wu_ref[...]), 0.0))
    acc_ref[...] += _dot(u.astype(MXU_DTYPE), wd_ref[...])

    @pl.when(f == pl.num_programs(1) - 1)
    def _():
        o_ref[...] = x_ref[...] + g_ref[...] * _rms(acc_ref[...], gout_ref[...])


def _mlp(x, gain_in, sc, sh, w_up, w_down, gain_out, g2):
    s, d = x.shape
    tm = min(s, 512)
    tf = 1024
    vec = pl.BlockSpec((1, d), lambda i, f: (0, 0))
    row = pl.BlockSpec((tm, d), lambda i, f: (i, 0))
    return pl.pallas_call(
        _mlp_kernel,
        grid=(s // tm, D_FF // tf),
        in_specs=[row, vec, vec, vec, pl.BlockSpec((d, tf), lambda i, f: (0, f)), pl.BlockSpec((tf, d), lambda i, f: (f, 0)), vec, vec],
        out_specs=row,
        out_shape=jax.ShapeDtypeStruct((s, d), F32),
        scratch_shapes=[pltpu.VMEM((tm, d), MXU_DTYPE), pltpu.VMEM((tm, d), F32)],
        compiler_params=_params("parallel", "arbitrary"),
        name="mlp",
    )(x, gain_in, sc, sh, w_up, w_down, gain_out, g2)


def _gate_weight(w_in_l):
    w = w_in_l[:, COL_GATE:COL_GA].reshape(D_MODEL, N_KV_GROUPS, GATES_PER_GROUP)
    w = jnp.pad(w, ((0, 0), (0, 0), (0, HEAD_DIM - GATES_PER_GROUP)))
    return w.reshape(D_MODEL, N_KV_GROUPS * HEAD_DIM).astype(MXU_DTYPE)


def kernel(x, c, positions, ada_w, ada_b, norm_gains, w_in, conv_w, w_conv_out, cmp_pe, cmp_w1, cmp_b1, cmp_w2, cmp_b2,
           w_nsa_out, w_out, w_mlp_up, w_mlp_down):
    b, s, d = x.shape
    assert b == 1 and d == D_MODEL
    depth = ada_w.shape[0]
    cast = lambda w: w.astype(MXU_DTYPE)
    xs = x.reshape(s, d)
    mod = _adaln(c, ada_w, ada_b)
    tables = _rope_tables(positions)
    ks_col = COL_KV + 2 * KV_WIDTH
    for l in range(depth):
        sh1, sc1, g1, sh2, sc2, g2 = [mod[l, k * d:(k + 1) * d].reshape(1, d) for k in range(6)]
        gains = [norm_gains[l, k].reshape(1, d) for k in range(4)]
        wl = w_in[l]
        h = _norm_mod(xs, gains[0], sc1, sh1)
        cf = _conv_proj(h, cast(wl[:, :COL_Q]), conv_w[l])
        w_qkv = cast(jnp.concatenate([wl[:, COL_Q:COL_KV], wl[:, ks_col:COL_GATE]], axis=1))
        qkv = _head_proj(h, w_qkv, tables, (0, 1, 2, 3, 4, 6), MXU_DTYPE)
        kv_raw = _head_proj(h, cast(wl[:, COL_KV:ks_col]), tables, (0,), F32)
        gates = _gate_proj(h, _gate_weight(wl))
        kvc = _compress(kv_raw, cmp_pe[l], cast(cmp_w1[l]), cmp_b1[l], cast(cmp_w2[l]), cmp_b2[l])
        ocw, sel = _cmp_win(qkv, kvc, gates, s)
        attn = _sel_attention(qkv, sel, ocw, gates, s)
        merged = _merge(h, cf, attn, cast(wl[:, COL_GA:COL_GB]), cast(wl[:, COL_GB:]), cast(w_conv_out[l]), cast(w_nsa_out[l]))
        xs = _out_proj(merged, cast(w_out[l]), xs, gains[1], g1)
        xs = _mlp(xs, gains[2], sc2, sh2, cast(w_mlp_up[l]), cast(w_mlp_down[l]), gains[3], g2)
    return xs.reshape(b, s, d)
```

```python
import functools

import jax
import jax.numpy as jnp
from jax import lax
from jax.experimental import pallas as pl
from jax.experimental.pallas import tpu as pltpu

F32 = jnp.float32
MXU_DTYPE = jnp.bfloat16

D_MODEL = 2048
CONV_WIDTH = D_MODEL // 2
N_HEADS = 16
N_KV_GROUPS = 4
HEADS_PER_GROUP = N_HEADS // N_KV_GROUPS
HEAD_DIM = D_MODEL // N_HEADS
KV_WIDTH = N_KV_GROUPS * HEAD_DIM
ROPE_DIM = HEAD_DIM // 4
ROPE_THETA = 500000.0
CMP_BLOCK = 32
CMP_STRIDE = 16
CMP_HIDDEN = 2 * HEAD_DIM
SEL_BLOCK = 64
SEL_SHIFT = 6
SEL_TOP = 16
WINDOW = 512
D_FF = 4 * D_MODEL
NORM_EPS = 1e-6
NEG = -1e30
FORCE = 1e30
REMOVED = -3e38
SCALE = HEAD_DIM ** -0.5
EXP2_SCALE = SCALE * 1.4426950408889634

COL_Q = 3 * CONV_WIDTH
COL_KV = COL_Q + N_HEADS * HEAD_DIM
COL_GATE = COL_KV + 6 * KV_WIDTH
COL_GA = COL_GATE + 3 * N_HEADS
COL_GB = COL_GA + D_MODEL
GATES_PER_GROUP = 3 * HEADS_PER_GROUP

VMEM_LIMIT_BYTES = 56 * 1024 * 1024


def _params(*semantics):
    return pltpu.CompilerParams(dimension_semantics=semantics, vmem_limit_bytes=VMEM_LIMIT_BYTES)


def _dot(a, b):
    return jnp.dot(a, b, preferred_element_type=F32)


def _dot_nt(a, b):
    return lax.dot_general(a, b, (((1,), (1,)), ((), ())), preferred_element_type=F32)


def _rms(y, gain):
    ms = jnp.mean(y * y, axis=-1, keepdims=True)
    return y * lax.rsqrt(ms + NORM_EPS) * gain


def _adaln_kernel(c_ref, w_ref, b_ref, o_ref):
    c = c_ref[...]
    act = (c * jax.nn.sigmoid(c)).astype(MXU_DTYPE)
    o_ref[0] = _dot(act, w_ref[0].astype(MXU_DTYPE)) + b_ref[0]


def _adaln(c, ada_w, ada_b):
    depth, d, n = ada_w.shape
    tn = 1024
    c8 = jnp.broadcast_to(c, (8, d))
    out = pl.pallas_call(
        _adaln_kernel,
        grid=(depth, n // tn),
        in_specs=[
            pl.BlockSpec((8, d), lambda l, j: (0, 0)),
            pl.BlockSpec((1, d, tn), lambda l, j: (l, 0, j)),
            pl.BlockSpec((1, 1, tn), lambda l, j: (l, 0, j)),
        ],
        out_specs=pl.BlockSpec((1, 8, tn), lambda l, j: (l, 0, j)),
        out_shape=jax.ShapeDtypeStruct((depth, 8, n), F32),
        compiler_params=_params("parallel", "parallel"),
        name="adaln",
    )(c8, ada_w, ada_b.reshape(depth, 1, n))
    return out[:, 0, :]


def _rope_table_kernel(pos_ref, freq_ref, c_ref, s1_ref, s2_ref):
    half = ROPE_DIM // 2
    ang = pos_ref[...].astype(F32) * freq_ref[...]
    lane = lax.broadcasted_iota(jnp.int32, ang.shape, 1)
    cos = jnp.cos(ang)
    sin = jnp.sin(ang)
    c_ref[...] = jnp.where(lane < ROPE_DIM, cos, 1.0)
    s1_ref[...] = jnp.where(lane < half, -sin, 0.0)
    s2_ref[...] = jnp.where((lane >= half) & (lane < ROPE_DIM), sin, 0.0)


def _rope_tables(positions):
    s = positions.shape[-1]
    tm = min(s, 2048)
    half = ROPE_DIM // 2
    inv_freq = ROPE_THETA ** (-jnp.arange(0, ROPE_DIM, 2, dtype=F32) / ROPE_DIM)
    freq = jnp.concatenate([inv_freq, inv_freq, jnp.zeros((HEAD_DIM - 2 * half,), F32)]).reshape(1, HEAD_DIM)
    spec = pl.BlockSpec((tm, HEAD_DIM), lambda i: (i, 0))
    shape = jax.ShapeDtypeStruct((s, HEAD_DIM), F32)
    return pl.pallas_call(
        _rope_table_kernel,
        grid=(s // tm,),
        in_specs=[pl.BlockSpec((tm, 1), lambda i: (i, 0)), pl.BlockSpec((1, HEAD_DIM), lambda i: (0, 0))],
        out_specs=[spec, spec, spec],
        out_shape=[shape, shape, shape],
        compiler_params=_params("parallel"),
        name="rope_tables",
    )(positions.reshape(s, 1), freq)


def _rope(x, c, s1, s2):
    half = ROPE_DIM // 2
    return x * c + pltpu.roll(x, HEAD_DIM - half, axis=1) * s1 + pltpu.roll(x, half, axis=1) * s2


def _norm_mod_kernel(x_ref, g_ref, sc_ref, sh_ref, o_ref):
    y = _rms(x_ref[...], g_ref[...])
    o_ref[...] = (y * (1.0 + sc_ref[...]) + sh_ref[...]).astype(o_ref.dtype)


def _norm_mod(x, gain, sc, sh):
    s, d = x.shape
    tm = min(s, 512)
    vec = pl.BlockSpec((1, d), lambda i: (0, 0))
    return pl.pallas_call(
        _norm_mod_kernel,
        grid=(s // tm,),
        in_specs=[pl.BlockSpec((tm, d), lambda i: (i, 0)), vec, vec, vec],
        out_specs=pl.BlockSpec((tm, d), lambda i: (i, 0)),
        out_shape=jax.ShapeDtypeStruct((s, d), MXU_DTYPE),
        compiler_params=_params("parallel"),
        name="norm_mod",
    )(x, gain, sc, sh)


def _conv_proj_kernel(h_ref, wb_ref, wc_ref, wx_ref, cw_ref, o_ref, carry_ref):
    i = pl.program_id(0)
    j = pl.program_id(1)
    tm = h_ref.shape[0]
    h = h_ref[...]
    u = _dot(h, wc_ref[...]) * _dot(h, wx_ref[...])

    @pl.when(i == 0)
    def _():
        carry_ref[j] = jnp.zeros(carry_ref.shape[1:], F32)

    prev = carry_ref[j]
    carry_ref[j] = u[tm - 8:, :]
    row = lax.broadcasted_iota(jnp.int32, u.shape, 0)
    u1 = jnp.where(row == 0, prev[7:8, :], pltpu.roll(u, 1, axis=0))
    u2 = jnp.where(row == 0, prev[6:7, :], jnp.where(row == 1, prev[7:8, :], pltpu.roll(u, 2, axis=0)))
    cw = cw_ref[...]
    z = cw[2:3, :] * u + cw[1:2, :] * u1 + cw[0:1, :] * u2
    o_ref[...] = (_dot(h, wb_ref[...]) * z).astype(o_ref.dtype)


def _conv_proj(h, w_conv, conv_w):
    s, d = h.shape
    tm = min(s, 1024)
    tn = 512
    nb = CONV_WIDTH // tn
    return pl.pallas_call(
        _conv_proj_kernel,
        grid=(s // tm, nb),
        in_specs=[
            pl.BlockSpec((tm, d), lambda i, j: (i, 0)),
            pl.BlockSpec((d, tn), lambda i, j: (0, j)),
            pl.BlockSpec((d, tn), lambda i, j: (0, j + nb)),
            pl.BlockSpec((d, tn), lambda i, j: (0, j + 2 * nb)),
            pl.BlockSpec((3, tn), lambda i, j: (0, j)),
        ],
        out_specs=pl.BlockSpec((tm, tn), lambda i, j: (i, j)),
        out_shape=jax.ShapeDtypeStruct((s, CONV_WIDTH), MXU_DTYPE),
        scratch_shapes=[pltpu.VMEM((nb, 8, tn), F32)],
        compiler_params=_params("arbitrary", "arbitrary"),
        name="conv_proj",
    )(h, w_conv, w_conv, w_conv, conv_w)


def _head_proj_kernel(h_ref, w_ref, c_ref, s1_ref, s2_ref, o_ref, *, rope_groups):
    j = pl.program_id(1)
    acc = _dot(h_ref[...], w_ref[...])
    is_rope = functools.reduce(jnp.logical_or, [j == r for r in rope_groups])
    c = jnp.where(is_rope, c_ref[...], 1.0)
    s1 = jnp.where(is_rope, s1_ref[...], 0.0)
    s2 = jnp.where(is_rope, s2_ref[...], 0.0)
    for r in range(HEADS_PER_GROUP):
        o_ref[r] = _rope(acc[:, r * HEAD_DIM:(r + 1) * HEAD_DIM], c, s1, s2).astype(o_ref.dtype)


def _head_proj(h, w, tables, rope_groups, out_dtype):
    s, d = h.shape
    tm = min(s, 2048)
    tn = HEADS_PER_GROUP * HEAD_DIM
    ng = w.shape[1] // tn
    tab = pl.BlockSpec((tm, HEAD_DIM), lambda i, j: (i, 0))
    return pl.pallas_call(
        functools.partial(_head_proj_kernel, rope_groups=rope_groups),
        grid=(s // tm, ng),
        in_specs=[pl.BlockSpec((tm, d), lambda i, j: (i, 0)), pl.BlockSpec((d, tn), lambda i, j: (0, j)), tab, tab, tab],
        out_specs=pl.BlockSpec((HEADS_PER_GROUP, tm, HEAD_DIM), lambda i, j: (j, i, 0)),
        out_shape=jax.ShapeDtypeStruct((ng * HEADS_PER_GROUP, s, HEAD_DIM), out_dtype),
        compiler_params=_params("parallel", "parallel"),
        name="head_proj",
    )(h, w, *tables)


def _gate_proj_kernel(h_ref, w_ref, o_ref):
    o_ref[...] = jax.nn.sigmoid(_dot(h_ref[...], w_ref[...]))


def _gate_proj(h, w_gate):
    s, d = h.shape
    tm = min(s, 1024)
    n = w_gate.shape[1]
    return pl.pallas_call(
        _gate_proj_kernel,
        grid=(s // tm,),
        in_specs=[pl.BlockSpec((tm, d), lambda i: (i, 0)), pl.BlockSpec((d, n), lambda i: (0, 0))],
        out_specs=pl.BlockSpec((tm, n), lambda i: (i, 0)),
        out_shape=jax.ShapeDtypeStruct((s, n), F32),
        compiler_params=_params("parallel"),
        name="gate_proj",
    )(h, w_gate)


def _compress_kernel(x_ref, pe_ref, w1_ref, b1_ref, w2_ref, b2_ref, o_ref):
    n = o_ref.shape[1]
    k_lo = CMP_STRIDE * HEAD_DIM
    pe = pe_ref[0]
    lo, hi = [], []
    for l in range(CMP_STRIDE):
        rows = x_ref[0, pl.ds(l, n, stride=CMP_STRIDE), :]
        lo.append((rows + pe[l:l + 1, :]).astype(MXU_DTYPE))
        hi.append((rows + pe[CMP_STRIDE + l:CMP_STRIDE + l + 1, :]).astype(MXU_DTYPE))
    t_lo = _dot(jnp.concatenate(lo, axis=1), w1_ref[0, :k_lo, :])
    t_hi = _dot(jnp.concatenate(hi, axis=1), w1_ref[0, k_lo:, :])
    hid = t_lo + pltpu.roll(t_hi, n - 1, axis=0) + b1_ref[0]
    out = _dot(jax.nn.gelu(hid).astype(MXU_DTYPE), w2_ref[0]) + b2_ref[0]
    row = lax.broadcasted_iota(jnp.int32, out.shape, 0)
    o_ref[0] = jnp.where(row == n - 1, 0.0, out).astype(o_ref.dtype)


def _compress(kv_raw, pe, w1, b1, w2, b2):
    nh, s, dh = kv_raw.shape
    n = s // CMP_STRIDE
    g = N_KV_GROUPS
    return pl.pallas_call(
        _compress_kernel,
        grid=(nh,),
        in_specs=[
            pl.BlockSpec((1, s, dh), lambda a: (a, 0, 0)),
            pl.BlockSpec((1, CMP_BLOCK, dh), lambda a: (a // g, 0, 0)),
            pl.BlockSpec((1, CMP_BLOCK * dh, CMP_HIDDEN), lambda a: (a // g, 0, 0)),
            pl.BlockSpec((1, 1, CMP_HIDDEN), lambda a: (a // g, 0, 0)),
            pl.BlockSpec((1, CMP_HIDDEN, dh), lambda a: (a // g, 0, 0)),
            pl.BlockSpec((1, 1, dh), lambda a: (a // g, 0, 0)),
        ],
        out_specs=pl.BlockSpec((1, n, dh), lambda a: (a, 0, 0)),
        out_shape=jax.ShapeDtypeStruct((nh, n, dh), MXU_DTYPE),
        compiler_params=_params("parallel"),
        name="compress",
    )(kv_raw, pe, w1, b1.reshape(2, 1, CMP_HIDDEN), w2, b2.reshape(2, 1, dh))


def _softmax_numerators(s, bias):
    sm = s + bias
    m = jnp.max(sm, axis=-1, keepdims=True)
    m = jnp.where(m < 0.5 * NEG, 0.0, m)
    return jnp.exp2(EXP2_SCALE * (sm - m)).astype(MXU_DTYPE)


def _cmp_win_kernel(q_ref, kc_ref, vc_ref, kw2_ref, kw1_ref, kw0_ref, vw2_ref, vw1_ref, vw0_ref, ov_ref, gate_ref,
                    ocw_ref, imp_ref, *, tq):
    i = pl.program_id(1)
    r_heads = HEADS_PER_GROUP
    s0 = i * tq
    q4 = q_ref[...].reshape(r_heads * tq, HEAD_DIM)
    tok = s0 + lax.broadcasted_iota(jnp.int32, (tq, 1), 0)

    kc = kc_ref[0]
    n_cmp = kc.shape[0]
    s_c = _dot_nt(q4, kc)
    cmp_end = lax.broadcasted_iota(jnp.int32, (1, n_cmp), 1) * CMP_STRIDE + (CMP_BLOCK - 1)
    bias_c = jnp.where(cmp_end <= tok, 0.0, NEG)
    p_c = jnp.concatenate([_softmax_numerators(s_c[r * tq:(r + 1) * tq, :], bias_c) for r in range(r_heads)], axis=0)
    ones_c = jnp.ones((n_cmp, HEAD_DIM), MXU_DTYPE)
    res_c = _dot(p_c, jnp.concatenate([vc_ref[0], ones_c, ov_ref[...]], axis=1))
    o_c, imp = [], None
    for r in range(r_heads):
        rows = slice(r * tq, (r + 1) * tq)
        l = res_c[rows, HEAD_DIM:2 * HEAD_DIM]
        inv = jnp.where(l > 0.0, 1.0 / l, 0.0)
        o_c.append(res_c[rows, :HEAD_DIM] * inv)
        part = res_c[rows, 2 * HEAD_DIM:] * inv[:, :1]
        imp = part if imp is None else imp + part
    imp_ref[0] = imp

    kwin = jnp.concatenate([kw2_ref[0], kw1_ref[0], kw0_ref[0]], axis=0)
    vwin = jnp.concatenate([vw2_ref[0], vw1_ref[0], vw0_ref[0]], axis=0)
    s_w = _dot_nt(q4, kwin)
    wpos = s0 - 2 * tq + lax.broadcasted_iota(jnp.int32, (1, 3 * tq), 1)
    diff = tok - wpos
    bias_w = jnp.where((diff >= 0) & (diff < WINDOW) & (wpos >= 0), 0.0, NEG)
    p_w = jnp.concatenate([_softmax_numerators(s_w[r * tq:(r + 1) * tq, :], bias_w) for r in range(r_heads)], axis=0)
    res_w = _dot(p_w, jnp.concatenate([vwin, jnp.ones((3 * tq, HEAD_DIM), MXU_DTYPE)], axis=1))

    gate = gate_ref[...]
    for r in range(r_heads):
        rows = slice(r * tq, (r + 1) * tq)
        g_cmp = gate[:, 3 * r:3 * r + 1]
        g_win = gate[:, 3 * r + 2:3 * r + 3]
        o_w = res_w[rows, :HEAD_DIM] / res_w[rows, HEAD_DIM:]
        ocw_ref[:, r * HEAD_DIM:(r + 1) * HEAD_DIM] = g_cmp * o_c[r] + g_win * o_w


def _overlap_matrix(n_cmp_rows, n_slc):
    cs = jnp.arange(n_cmp_rows)[:, None] * CMP_STRIDE
    ss = jnp.arange(n_slc)[None, :] * SEL_BLOCK
    ov = jnp.clip(jnp.minimum(cs + CMP_BLOCK, ss + SEL_BLOCK) - jnp.maximum(cs, ss), 0, None)
    return (ov.astype(F32) / CMP_BLOCK).astype(MXU_DTYPE)


def _cmp_win(qkv, kvc, gates, s):
    tq = min(s, 256)
    assert WINDOW <= 2 * tq
    g = N_KV_GROUPS
    n_cmp = kvc.shape[1]
    n_slc = s // SEL_BLOCK
    head = lambda base: [
        pl.BlockSpec((1, tq, HEAD_DIM), functools.partial(lambda a, i, d, base: (base + a, jnp.maximum(i - d, 0), 0), d=d, base=base))
        for d in (2, 1, 0)
    ]
    ocw, imp = pl.pallas_call(
        functools.partial(_cmp_win_kernel, tq=tq),
        grid=(g, s // tq),
        in_specs=[
            pl.BlockSpec((HEADS_PER_GROUP, tq, HEAD_DIM), lambda a, i: (a, i, 0)),
            pl.BlockSpec((1, n_cmp, HEAD_DIM), lambda a, i: (a, 0, 0)),
            pl.BlockSpec((1, n_cmp, HEAD_DIM), lambda a, i: (g + a, 0, 0)),
            *head(N_HEADS + 2 * g),
            *head(N_HEADS + 3 * g),
            pl.BlockSpec((n_cmp, n_slc), lambda a, i: (0, 0)),
            pl.BlockSpec((tq, HEAD_DIM), lambda a, i: (i, a)),
        ],
        out_specs=[
            pl.BlockSpec((tq, HEADS_PER_GROUP * HEAD_DIM), lambda a, i: (i, a)),
            pl.BlockSpec((1, tq, n_slc), lambda a, i: (a, i, 0)),
        ],
        out_shape=[
            jax.ShapeDtypeStruct((s, N_HEADS * HEAD_DIM), F32),
            jax.ShapeDtypeStruct((g, s, n_slc), F32),
        ],
        compiler_params=_params("parallel", "parallel"),
        name="cmp_win",
    )(qkv, kvc, kvc, qkv, qkv, qkv, qkv, qkv, qkv, _overlap_matrix(n_cmp, n_slc), gates)
    return ocw, imp


def _topk_kernel(imp_ref, sb_ref, work_ref):
    tm, n_slc = work_ref.shape
    tok = pl.program_id(1) * tm + lax.broadcasted_iota(jnp.int32, (tm, 1), 0)
    blk = lax.broadcasted_iota(jnp.int32, (1, n_slc), 1)
    cur = tok >> SEL_SHIFT
    forced = (blk == cur) | (blk == 0)
    work_ref[...] = jnp.where(forced, FORCE, jnp.where(blk <= cur, imp_ref[0], NEG))
    blk_f = blk.astype(F32)
    for _ in range(min(SEL_TOP, n_slc)):
        w = work_ref[...]
        best = jnp.max(w, axis=-1, keepdims=True)
        idx = jnp.min(jnp.where(w == best, blk_f, float(n_slc)), axis=-1, keepdims=True)
        work_ref[...] = jnp.where(blk_f == idx, REMOVED, w)
    sb_ref[0] = jnp.where(work_ref[...] == REMOVED, 0.0, NEG).astype(sb_ref.dtype)


def _topk_bias(imp):
    g, s, n_slc = imp.shape
    tm = min(s, 1024)
    spec = pl.BlockSpec((1, tm, n_slc), lambda a, i: (a, i, 0))
    return pl.pallas_call(
        _topk_kernel,
        grid=(g, s // tm),
        in_specs=[spec],
        out_specs=spec,
        out_shape=jax.ShapeDtypeStruct((g, s, n_slc), MXU_DTYPE),
        scratch_shapes=[pltpu.VMEM((tm, n_slc), F32)],
        compiler_params=_params("parallel", "parallel"),
        name="topk_bias",
    )(imp)


def _sel_kernel(q_ref, k_ref, v_ref, sb_ref, ocw_ref, gate_ref, o_ref, lhs_ref, s_ref, m_ref, acc_ref, *, tq, tk):
    i = pl.program_id(1)
    r_heads = HEADS_PER_GROUP
    bias_lanes = lhs_ref.shape[2] - HEAD_DIM
    blocks_per_tile = tk // SEL_BLOCK
    lane_tiles = tk // HEAD_DIM
    n_tiles = (i * tq + tq - 1) // tk + 1
    tok = i * tq + lax.broadcasted_iota(jnp.int32, (tq, 1), 0)
    key_blk = lax.broadcasted_iota(jnp.int32, (tk, bias_lanes), 0) >> SEL_SHIFT
    lane = lax.broadcasted_iota(jnp.int32, (tk, bias_lanes), 1)
    key_off = lax.broadcasted_iota(jnp.int32, (1, tk), 1)

    for c in range(lhs_ref.shape[0]):
        for r in range(r_heads):
            rows = slice(r * tq, (r + 1) * tq)
            lhs_ref[c, rows, :HEAD_DIM] = q_ref[r]
            lhs_ref[c, rows, HEAD_DIM:] = sb_ref[0, :, c * bias_lanes:(c + 1) * bias_lanes]

    def produce(slot, j):
        start = pl.multiple_of(j * tk, tk)
        first_blk = j * blocks_per_tile
        onehot = jnp.where(lane == first_blk % bias_lanes + key_blk, 1.0, 0.0).astype(MXU_DTYPE)
        k_ext = jnp.concatenate([k_ref[0, pl.ds(start, tk), :], onehot], axis=1)
        s_ref[slot] = _dot_nt(lhs_ref[first_blk // bias_lanes], k_ext)

    def consume(slot, j, diagonal):
        start = pl.multiple_of(j * tk, tk)
        v_ext = jnp.concatenate([v_ref[0, pl.ds(start, tk), :], jnp.ones((tk, HEAD_DIM), MXU_DTYPE)], axis=1)
        p_rows, alphas = [], []
        for r in range(r_heads):
            rows = slice(r * tq, (r + 1) * tq)
            sm = s_ref[slot, rows, :]
            if diagonal:
                sm = jnp.where(j * tk + key_off <= tok, sm, NEG)
            m_prev = m_ref[rows, :]
            m_new = jnp.maximum(m_prev, jnp.max(sm, axis=-1, keepdims=True))
            m_ref[rows, :] = m_new
            p_rows.append(jnp.exp2(EXP2_SCALE * (sm - pltpu.repeat(m_new, lane_tiles, axis=1))).astype(MXU_DTYPE))
            alphas.append(jnp.exp2(EXP2_SCALE * (m_prev - m_new)))
        pv = _dot(jnp.concatenate(p_rows, axis=0), v_ext)
        for r in range(r_heads):
            rows = slice(r * tq, (r + 1) * tq)
            acc_ref[rows, :] = acc_ref[rows, :] * pltpu.repeat(alphas[r], 2, axis=1) + pv[rows, :]

    m_ref[...] = jnp.full(m_ref.shape, NEG, F32)
    acc_ref[...] = jnp.zeros(acc_ref.shape, F32)
    produce(0, 0)

    def body(pair, carry):
        j = 2 * pair
        consume(0, j, diagonal=False)
        produce(1, j + 1)
        consume(1, j + 1, diagonal=False)
        produce(0, j + 2)
        return carry

    n_pairs = (n_tiles - 1) // 2
    lax.fori_loop(0, n_pairs, body, 0)
    done = 2 * n_pairs

    @pl.when(n_tiles - done == 1)
    def _():
        consume(0, done, diagonal=True)

    @pl.when(n_tiles - done == 2)
    def _():
        consume(0, done, diagonal=False)
        produce(1, done + 1)
        consume(1, done + 1, diagonal=True)

    gate = gate_ref[...]
    for r in range(r_heads):
        rows = slice(r * tq, (r + 1) * tq)
        cols = slice(r * HEAD_DIM, (r + 1) * HEAD_DIM)
        o_s = acc_ref[rows, :HEAD_DIM] / acc_ref[rows, HEAD_DIM:]
        o_ref[:, cols] = (ocw_ref[:, cols] + gate[:, 3 * r + 1:3 * r + 2] * o_s).astype(o_ref.dtype)


def _sel_attention(qkv, sel_bias, ocw, gates, s):
    tq = min(s, 512)
    tk = min(s, 512)
    g = N_KV_GROUPS
    n_slc = s // SEL_BLOCK
    width = HEADS_PER_GROUP * HEAD_DIM
    rows = HEADS_PER_GROUP * tq
    bias_lanes = min(HEAD_DIM, n_slc)
    assert n_slc % bias_lanes == 0 and bias_lanes % (tk // SEL_BLOCK) == 0 and tk % tq == 0
    return pl.pallas_call(
        functools.partial(_sel_kernel, tq=tq, tk=tk),
        grid=(g, s // tq),
        in_specs=[
            pl.BlockSpec((HEADS_PER_GROUP, tq, HEAD_DIM), lambda a, i: (a, i, 0)),
            pl.BlockSpec((1, s, HEAD_DIM), lambda a, i: (N_HEADS + a, 0, 0)),
            pl.BlockSpec((1, s, HEAD_DIM), lambda a, i: (N_HEADS + g + a, 0, 0)),
            pl.BlockSpec((1, tq, n_slc), lambda a, i: (a, i, 0)),
            pl.BlockSpec((tq, width), lambda a, i: (i, a)),
            pl.BlockSpec((tq, HEAD_DIM), lambda a, i: (i, a)),
        ],
        out_specs=pl.BlockSpec((tq, width), lambda a, i: (i, a)),
        out_shape=jax.ShapeDtypeStruct((s, N_HEADS * HEAD_DIM), MXU_DTYPE),
        scratch_shapes=[
            pltpu.VMEM((n_slc // bias_lanes, rows, HEAD_DIM + bias_lanes), MXU_DTYPE),
            pltpu.VMEM((2, rows, tk), F32),
            pltpu.VMEM((rows, HEAD_DIM), F32),
            pltpu.VMEM((rows, 2 * HEAD_DIM), F32),
        ],
        compiler_params=_params("parallel", "arbitrary"),
        name="sel_attention",
    )(qkv, qkv, qkv, sel_bias, ocw, gates)


def _merge_kernel(h_ref, cf_ref, at_ref, wga_ref, wgb_ref, wco_ref, wno_ref, o_ref):
    h = h_ref[...]
    y_a = _dot(cf_ref[...], wco_ref[...])
    y_b = _dot(at_ref[...], wno_ref[...])
    merged = jax.nn.sigmoid(_dot(h, wga_ref[...])) * y_a + jax.nn.sigmoid(_dot(h, wgb_ref[...])) * y_b
    o_ref[...] = merged.astype(o_ref.dtype)


def _merge(h, cf, attn, w_ga, w_gb, w_co, w_no):
    s, d = h.shape
    tm = min(s, 1024)
    tn = 512
    row = lambda width: pl.BlockSpec((tm, width), lambda i, j: (i, 0))
    col = lambda depth: pl.BlockSpec((depth, tn), lambda i, j: (0, j))
    return pl.pallas_call(
        _merge_kernel,
        grid=(s // tm, d // tn),
        in_specs=[row(d), row(CONV_WIDTH), row(N_HEADS * HEAD_DIM), col(d), col(d), col(CONV_WIDTH), col(N_HEADS * HEAD_DIM)],
        out_specs=pl.BlockSpec((tm, tn), lambda i, j: (i, j)),
        out_shape=jax.ShapeDtypeStruct((s, d), MXU_DTYPE),
        compiler_params=_params("parallel", "parallel"),
        name="merge",
    )(h, cf, attn, w_ga, w_gb, w_co, w_no)


def _out_proj_kernel(m_ref, w_ref, x_ref, gain_ref, g_ref, o_ref):
    y = _dot(m_ref[...], w_ref[...])
    o_ref[...] = x_ref[...] + g_ref[...] * _rms(y, gain_ref[...])


def _out_proj(merged, w_out, x, gain, g1):
    s, d = x.shape
    tm = min(s, 512)
    vec = pl.BlockSpec((1, d), lambda i: (0, 0))
    row = pl.BlockSpec((tm, d), lambda i: (i, 0))
    return pl.pallas_call(
        _out_proj_kernel,
        grid=(s // tm,),
        in_specs=[row, pl.BlockSpec((d, d), lambda i: (0, 0)), row, vec, vec],
        out_specs=row,
        out_shape=jax.ShapeDtypeStruct((s, d), F32),
        compiler_params=_params("parallel"),
        name="out_proj",
    )(merged, w_out, x, gain, g1)


def _mlp_kernel(x_ref, gin_ref, sc_ref, sh_ref, wu_ref, wd_ref, gout_ref, g_ref, o_ref, h_ref, acc_ref):
    f = pl.program_id(1)

    @pl.when(f == 0)
    def _():
        y = _rms(x_ref[...], gin_ref[...])
        h_ref[...] = (y * (1.0 + sc_ref[...]) + sh_ref[...]).astype(h_ref.dtype)
        acc_ref[...] = jnp.zeros(acc_ref.shape, F32)

    u = jnp.square(jnp.maximum(_dot(h_ref[...], wu_ref[...]), 0.0))
    acc_ref[...] += _dot(u.astype(MXU_DTYPE), wd_ref[...])

    @pl.when(f == pl.num_programs(1) - 1)
    def _():
        o_ref[...] = x_ref[...] + g_ref[...] * _rms(acc_ref[...], gout_ref[...])


def _mlp(x, gain_in, sc, sh, w_up, w_down, gain_out, g2):
    s, d = x.shape
    tm = min(s, 512)
    tf = 1024
    vec = pl.BlockSpec((1, d), lambda i, f: (0, 0))
    row = pl.BlockSpec((tm, d), lambda i, f: (i, 0))
    return pl.pallas_call(
        _mlp_kernel,
        grid=(s // tm, D_FF // tf),
        in_specs=[row, vec, vec, vec, pl.BlockSpec((d, tf), lambda i, f: (0, f)), pl.BlockSpec((tf, d), lambda i, f: (f, 0)), vec, vec],
        out_specs=row,
        out_shape=jax.ShapeDtypeStruct((s, d), F32),
        scratch_shapes=[pltpu.VMEM((tm, d), MXU_DTYPE), pltpu.VMEM((tm, d), F32)],
        compiler_params=_params("parallel", "arbitrary"),
        name="mlp",
    )(x, gain_in, sc, sh, w_up, w_down, gain_out, g2)


def _gate_weight(w_in_l):
    w = w_in_l[:, COL_GATE:COL_GA].reshape(D_MODEL, N_KV_GROUPS, GATES_PER_GROUP)
    w = jnp.pad(w, ((0, 0), (0, 0), (0, HEAD_DIM - GATES_PER_GROUP)))
    return w.reshape(D_MODEL, N_KV_GROUPS * HEAD_DIM).astype(MXU_DTYPE)


def kernel(x, c, positions, ada_w, ada_b, norm_gains, w_in, conv_w, w_conv_out, cmp_pe, cmp_w1, cmp_b1, cmp_w2, cmp_b2,
           w_nsa_out, w_out, w_mlp_up, w_mlp_down):
    b, s, d = x.shape
    assert b == 1 and d == D_MODEL
    depth = ada_w.shape[0]
    cast = lambda w: w.astype(MXU_DTYPE)
    xs = x.reshape(s, d)
    mod = _adaln(c, ada_w, ada_b)
    tables = _rope_tables(positions)
    ks_col = COL_KV + 2 * KV_WIDTH
    for l in range(depth):
        sh1, sc1, g1, sh2, sc2, g2 = [mod[l, k * d:(k + 1) * d].reshape(1, d) for k in range(6)]
        gains = [norm_gains[l, k].reshape(1, d) for k in range(4)]
        wl = w_in[l]
        h = _norm_mod(xs, gains[0], sc1, sh1)
        cf = _conv_proj(h, cast(wl[:, :COL_Q]), conv_w[l])
        w_qkv = cast(jnp.concatenate([wl[:, COL_Q:COL_KV], wl[:, ks_col:COL_GATE]], axis=1))
        qkv = _head_proj(h, w_qkv, tables, (0, 1, 2, 3, 4, 6), MXU_DTYPE)
        kv_raw = _head_proj(h, cast(wl[:, COL_KV:ks_col]), tables, (0,), F32)
        gates = _gate_proj(h, _gate_weight(wl))
        kvc = _compress(kv_raw, cmp_pe[l], cast(cmp_w1[l]), cmp_b1[l], cast(cmp_w2[l]), cmp_b2[l])
        ocw, imp = _cmp_win(qkv, kvc, gates, s)
        attn = _sel_attention(qkv, _topk_bias(imp), ocw, gates, s)
        merged = _merge(h, cf, attn, cast(wl[:, COL_GA:COL_GB]), cast(wl[:, COL_GB:]), cast(w_conv_out[l]), cast(w_nsa_out[l]))
        xs = _out_proj(merged, cast(w_out[l]), xs, gains[1], g1)
        xs = _mlp(xs, gains[2], sc2, sh2, cast(w_mlp_up[l]), cast(w_mlp_down[l]), gains[3], g2)
    return xs.reshape(b, s, d)
```

```python
import functools

import jax
import jax.numpy as jnp
from jax import lax
from jax.experimental import pallas as pl
from jax.experimental.pallas import tpu as pltpu

F32 = jnp.float32
MXU_DTYPE = jnp.bfloat16

D_MODEL = 2048
CONV_WIDTH = D_MODEL // 2
N_HEADS = 16
N_KV_GROUPS = 4
HEADS_PER_GROUP = N_HEADS // N_KV_GROUPS
HEAD_DIM = D_MODEL // N_HEADS
KV_WIDTH = N_KV_GROUPS * HEAD_DIM
ROPE_DIM = HEAD_DIM // 4
ROPE_THETA = 500000.0
CMP_BLOCK = 32
CMP_STRIDE = 16
CMP_HIDDEN = 2 * HEAD_DIM
SEL_BLOCK = 64
SEL_SHIFT = 6
SEL_TOP = 16
WINDOW = 512
D_FF = 4 * D_MODEL
NORM_EPS = 1e-6
NEG = -1e30
REMOVED = -3e38
SCALE = HEAD_DIM ** -0.5
EXP2_SCALE = SCALE * 1.4426950408889634

COL_Q = 3 * CONV_WIDTH
COL_KV = COL_Q + N_HEADS * HEAD_DIM
COL_GATE = COL_KV + 6 * KV_WIDTH
COL_GA = COL_GATE + 3 * N_HEADS
COL_GB = COL_GA + D_MODEL
GATES_PER_GROUP = 3 * HEADS_PER_GROUP

VMEM_LIMIT_BYTES = 56 * 1024 * 1024


def _params(*semantics):
    return pltpu.CompilerParams(dimension_semantics=semantics, vmem_limit_bytes=VMEM_LIMIT_BYTES)


def _dot(a, b):
    return jnp.dot(a, b, preferred_element_type=F32)


def _dot_nt(a, b):
    return lax.dot_general(a, b, (((1,), (1,)), ((), ())), preferred_element_type=F32)


def _rms(y, gain):
    ms = jnp.mean(y * y, axis=-1, keepdims=True)
    return y * lax.rsqrt(ms + NORM_EPS) * gain


def _adaln_kernel(c_ref, w_ref, b_ref, o_ref):
    c = c_ref[...]
    act = (c * jax.nn.sigmoid(c)).astype(MXU_DTYPE)
    o_ref[0] = _dot(act, w_ref[0].astype(MXU_DTYPE)) + b_ref[0]


def _adaln(c, ada_w, ada_b):
    depth, d, n = ada_w.shape
    tn = 1024
    c8 = jnp.broadcast_to(c, (8, d))
    out = pl.pallas_call(
        _adaln_kernel,
        grid=(depth, n // tn),
        in_specs=[
            pl.BlockSpec((8, d), lambda l, j: (0, 0)),
            pl.BlockSpec((1, d, tn), lambda l, j: (l, 0, j)),
            pl.BlockSpec((1, 1, tn), lambda l, j: (l, 0, j)),
        ],
        out_specs=pl.BlockSpec((1, 8, tn), lambda l, j: (l, 0, j)),
        out_shape=jax.ShapeDtypeStruct((depth, 8, n), F32),
        compiler_params=_params("parallel", "parallel"),
        name="adaln",
    )(c8, ada_w, ada_b.reshape(depth, 1, n))
    return out[:, 0, :]


def _rope_table_kernel(pos_ref, freq_ref, c_ref, s1_ref, s2_ref):
    half = ROPE_DIM // 2
    ang = pos_ref[...].astype(F32) * freq_ref[...]
    lane = lax.broadcasted_iota(jnp.int32, ang.shape, 1)
    cos = jnp.cos(ang)
    sin = jnp.sin(ang)
    c_ref[...] = jnp.where(lane < ROPE_DIM, cos, 1.0)
    s1_ref[...] = jnp.where(lane < half, -sin, 0.0)
    s2_ref[...] = jnp.where((lane >= half) & (lane < ROPE_DIM), sin, 0.0)


def _rope_tables(positions):
    s = positions.shape[-1]
    tm = min(s, 2048)
    half = ROPE_DIM // 2
    inv_freq = ROPE_THETA ** (-jnp.arange(0, ROPE_DIM, 2, dtype=F32) / ROPE_DIM)
    freq = jnp.concatenate([inv_freq, inv_freq, jnp.zeros((HEAD_DIM - 2 * half,), F32)]).reshape(1, HEAD_DIM)
    spec = pl.BlockSpec((tm, HEAD_DIM), lambda i: (i, 0))
    shape = jax.ShapeDtypeStruct((s, HEAD_DIM), F32)
    return pl.pallas_call(
        _rope_table_kernel,
        grid=(s // tm,),
        in_specs=[pl.BlockSpec((tm, 1), lambda i: (i, 0)), pl.BlockSpec((1, HEAD_DIM), lambda i: (0, 0))],
        out_specs=[spec, spec, spec],
        out_shape=[shape, shape, shape],
        compiler_params=_params("parallel"),
        name="rope_tables",
    )(positions.reshape(s, 1), freq)


def _rope(x, c, s1, s2):
    half = ROPE_DIM // 2
    return x * c + pltpu.roll(x, HEAD_DIM - half, axis=1) * s1 + pltpu.roll(x, half, axis=1) * s2


def _norm_mod_kernel(x_ref, g_ref, sc_ref, sh_ref, o_ref):
    y = _rms(x_ref[...], g_ref[...])
    o_ref[...] = (y * (1.0 + sc_ref[...]) + sh_ref[...]).astype(o_ref.dtype)


def _norm_mod(x, gain, sc, sh):
    s, d = x.shape
    tm = min(s, 512)
    vec = pl.BlockSpec((1, d), lambda i: (0, 0))
    return pl.pallas_call(
        _norm_mod_kernel,
        grid=(s // tm,),
        in_specs=[pl.BlockSpec((tm, d), lambda i: (i, 0)), vec, vec, vec],
        out_specs=pl.BlockSpec((tm, d), lambda i: (i, 0)),
        out_shape=jax.ShapeDtypeStruct((s, d), MXU_DTYPE),
        compiler_params=_params("parallel"),
        name="norm_mod",
    )(x, gain, sc, sh)


def _conv_proj_kernel(h_ref, wb_ref, wc_ref, wx_ref, cw_ref, o_ref, carry_ref):
    i = pl.program_id(0)
    j = pl.program_id(1)
    tm = h_ref.shape[0]
    h = h_ref[...]
    u = _dot(h, wc_ref[...]) * _dot(h, wx_ref[...])

    @pl.when(i == 0)
    def _():
        carry_ref[j] = jnp.zeros(carry_ref.shape[1:], F32)

    prev = carry_ref[j]
    carry_ref[j] = u[tm - 8:, :]
    row = lax.broadcasted_iota(jnp.int32, u.shape, 0)
    u1 = jnp.where(row == 0, prev[7:8, :], pltpu.roll(u, 1, axis=0))
    u2 = jnp.where(row == 0, prev[6:7, :], jnp.where(row == 1, prev[7:8, :], pltpu.roll(u, 2, axis=0)))
    cw = cw_ref[...]
    z = cw[2:3, :] * u + cw[1:2, :] * u1 + cw[0:1, :] * u2
    o_ref[...] = (_dot(h, wb_ref[...]) * z).astype(o_ref.dtype)


def _conv_proj(h, w_conv, conv_w):
    s, d = h.shape
    tm = min(s, 1024)
    tn = 512
    nb = CONV_WIDTH // tn
    return pl.pallas_call(
        _conv_proj_kernel,
        grid=(s // tm, nb),
        in_specs=[
            pl.BlockSpec((tm, d), lambda i, j: (i, 0)),
            pl.BlockSpec((d, tn), lambda i, j: (0, j)),
            pl.BlockSpec((d, tn), lambda i, j: (0, j + nb)),
            pl.BlockSpec((d, tn), lambda i, j: (0, j + 2 * nb)),
            pl.BlockSpec((3, tn), lambda i, j: (0, j)),
        ],
        out_specs=pl.BlockSpec((tm, tn), lambda i, j: (i, j)),
        out_shape=jax.ShapeDtypeStruct((s, CONV_WIDTH), MXU_DTYPE),
        scratch_shapes=[pltpu.VMEM((nb, 8, tn), F32)],
        compiler_params=_params("arbitrary", "arbitrary"),
        name="conv_proj",
    )(h, w_conv, w_conv, w_conv, conv_w)


def _head_proj_kernel(h_ref, w_ref, c_ref, s1_ref, s2_ref, o_ref, *, rope_groups):
    j = pl.program_id(1)
    groups_per_step = o_ref.shape[0] // HEADS_PER_GROUP
    width = HEADS_PER_GROUP * HEAD_DIM
    for k in range(groups_per_step):
        acc = _dot(h_ref[...], w_ref[:, k * width:(k + 1) * width])
        is_rope = functools.reduce(jnp.logical_or, [j * groups_per_step + k == r for r in rope_groups])
        c = jnp.where(is_rope, c_ref[...], 1.0)
        s1 = jnp.where(is_rope, s1_ref[...], 0.0)
        s2 = jnp.where(is_rope, s2_ref[...], 0.0)
        for r in range(HEADS_PER_GROUP):
            head = _rope(acc[:, r * HEAD_DIM:(r + 1) * HEAD_DIM], c, s1, s2)
            o_ref[k * HEADS_PER_GROUP + r] = head.astype(o_ref.dtype)


def _head_proj(h, w, tables, rope_groups, out_dtype):
    s, d = h.shape
    tm = min(s, 512)
    tn = min(w.shape[1], 4 * HEADS_PER_GROUP * HEAD_DIM)
    heads_per_step = tn // HEAD_DIM
    tab = pl.BlockSpec((tm, HEAD_DIM), lambda i, j: (i, 0))
    return pl.pallas_call(
        functools.partial(_head_proj_kernel, rope_groups=rope_groups),
        grid=(s // tm, w.shape[1] // tn),
        in_specs=[pl.BlockSpec((tm, d), lambda i, j: (i, 0)), pl.BlockSpec((d, tn), lambda i, j: (0, j)), tab, tab, tab],
        out_specs=pl.BlockSpec((heads_per_step, tm, HEAD_DIM), lambda i, j: (j, i, 0)),
        out_shape=jax.ShapeDtypeStruct((w.shape[1] // HEAD_DIM, s, HEAD_DIM), out_dtype),
        compiler_params=_params("parallel", "parallel"),
        name="head_proj",
    )(h, w, *tables)


def _gate_proj_kernel(h_ref, w_ref, o_ref):
    o_ref[...] = jax.nn.sigmoid(_dot(h_ref[...], w_ref[...]))


def _gate_proj(h, w_gate):
    s, d = h.shape
    tm = min(s, 1024)
    n = w_gate.shape[1]
    return pl.pallas_call(
        _gate_proj_kernel,
        grid=(s // tm,),
        in_specs=[pl.BlockSpec((tm, d), lambda i: (i, 0)), pl.BlockSpec((d, n), lambda i: (0, 0))],
        out_specs=pl.BlockSpec((tm, n), lambda i: (i, 0)),
        out_shape=jax.ShapeDtypeStruct((s, n), F32),
        compiler_params=_params("parallel"),
        name="gate_proj",
    )(h, w_gate)


def _compress_kernel(x_ref, pe_ref, w1_ref, b1_ref, w2_ref, b2_ref, o_ref):
    n = o_ref.shape[1]
    k_lo = CMP_STRIDE * HEAD_DIM
    pe = pe_ref[0]
    lo, hi = [], []
    for l in range(CMP_STRIDE):
        rows = x_ref[0, pl.ds(l, n, stride=CMP_STRIDE), :]
        lo.append((rows + pe[l:l + 1, :]).astype(MXU_DTYPE))
        hi.append((rows + pe[CMP_STRIDE + l:CMP_STRIDE + l + 1, :]).astype(MXU_DTYPE))
    t_lo = _dot(jnp.concatenate(lo, axis=1), w1_ref[0, :k_lo, :])
    t_hi = _dot(jnp.concatenate(hi, axis=1), w1_ref[0, k_lo:, :])
    hid = t_lo + pltpu.roll(t_hi, n - 1, axis=0) + b1_ref[0]
    out = _dot(jax.nn.gelu(hid).astype(MXU_DTYPE), w2_ref[0]) + b2_ref[0]
    row = lax.broadcasted_iota(jnp.int32, out.shape, 0)
    o_ref[0] = jnp.where(row == n - 1, 0.0, out).astype(o_ref.dtype)


def _compress(kv_raw, pe, w1, b1, w2, b2):
    nh, s, dh = kv_raw.shape
    n = s // CMP_STRIDE
    g = N_KV_GROUPS
    return pl.pallas_call(
        _compress_kernel,
        grid=(nh,),
        in_specs=[
            pl.BlockSpec((1, s, dh), lambda a: (a, 0, 0)),
            pl.BlockSpec((1, CMP_BLOCK, dh), lambda a: (a // g, 0, 0)),
            pl.BlockSpec((1, CMP_BLOCK * dh, CMP_HIDDEN), lambda a: (a // g, 0, 0)),
            pl.BlockSpec((1, 1, CMP_HIDDEN), lambda a: (a // g, 0, 0)),
            pl.BlockSpec((1, CMP_HIDDEN, dh), lambda a: (a // g, 0, 0)),
            pl.BlockSpec((1, 1, dh), lambda a: (a // g, 0, 0)),
        ],
        out_specs=pl.BlockSpec((1, n, dh), lambda a: (a, 0, 0)),
        out_shape=jax.ShapeDtypeStruct((nh, n, dh), MXU_DTYPE),
        compiler_params=_params("parallel"),
        name="compress",
    )(kv_raw, pe, w1, b1.reshape(2, 1, CMP_HIDDEN), w2, b2.reshape(2, 1, dh))


def _softmax_numerators(s, bias):
    sm = s + bias
    m = jnp.max(sm, axis=-1, keepdims=True)
    m = jnp.where(m < 0.5 * NEG, 0.0, m)
    return jnp.exp2(EXP2_SCALE * (sm - m)).astype(MXU_DTYPE)


def _cmp_win_kernel(q_ref, kc_ref, vc_ref, kw2_ref, kw1_ref, kw0_ref, vw2_ref, vw1_ref, vw0_ref, ov_ref, gate_ref,
                    ocw_ref, imp_ref, *, tq, chunk):
    n_cmp = kc_ref.shape[1]
    last_visible = ((pl.program_id(1) + 1) * tq - CMP_BLOCK) // CMP_STRIDE
    for c in range(n_cmp // chunk):
        pl.when(jnp.maximum(last_visible, 0) // chunk == c)(functools.partial(
            _cmp_win_body, q_ref, kc_ref, vc_ref, kw2_ref, kw1_ref, kw0_ref, vw2_ref, vw1_ref, vw0_ref, ov_ref, gate_ref,
            ocw_ref, imp_ref, tq=tq, n_cmp=(c + 1) * chunk))


def _cmp_win_body(q_ref, kc_ref, vc_ref, kw2_ref, kw1_ref, kw0_ref, vw2_ref, vw1_ref, vw0_ref, ov_ref, gate_ref,
                  ocw_ref, imp_ref, *, tq, n_cmp):
    i = pl.program_id(1)
    r_heads = HEADS_PER_GROUP
    s0 = i * tq
    q4 = q_ref[...].reshape(r_heads * tq, HEAD_DIM)
    tok = s0 + lax.broadcasted_iota(jnp.int32, (tq, 1), 0)

    s_c = _dot_nt(q4, kc_ref[0, :n_cmp, :])
    cmp_end = lax.broadcasted_iota(jnp.int32, (1, n_cmp), 1) * CMP_STRIDE + (CMP_BLOCK - 1)
    bias_c = jnp.where(cmp_end <= tok, 0.0, NEG)
    p_c = jnp.concatenate([_softmax_numerators(s_c[r * tq:(r + 1) * tq, :], bias_c) for r in range(r_heads)], axis=0)
    ones_c = jnp.ones((n_cmp, HEAD_DIM), MXU_DTYPE)
    res_c = _dot(p_c, jnp.concatenate([vc_ref[0, :n_cmp, :], ones_c, ov_ref[:n_cmp, :]], axis=1))
    o_c, imp = [], None
    for r in range(r_heads):
        rows = slice(r * tq, (r + 1) * tq)
        l = res_c[rows, HEAD_DIM:2 * HEAD_DIM]
        inv = jnp.where(l > 0.0, 1.0 / l, 0.0)
        o_c.append(res_c[rows, :HEAD_DIM] * inv)
        part = res_c[rows, 2 * HEAD_DIM:] * inv[:, :1]
        imp = part if imp is None else imp + part
    imp_ref[0] = imp

    kwin = jnp.concatenate([kw2_ref[0], kw1_ref[0], kw0_ref[0]], axis=0)
    vwin = jnp.concatenate([vw2_ref[0], vw1_ref[0], vw0_ref[0]], axis=0)
    s_w = _dot_nt(q4, kwin)
    wpos = s0 - 2 * tq + lax.broadcasted_iota(jnp.int32, (1, 3 * tq), 1)
    diff = tok - wpos
    bias_w = jnp.where((diff >= 0) & (diff < WINDOW) & (wpos >= 0), 0.0, NEG)
    p_w = jnp.concatenate([_softmax_numerators(s_w[r * tq:(r + 1) * tq, :], bias_w) for r in range(r_heads)], axis=0)
    res_w = _dot(p_w, jnp.concatenate([vwin, jnp.ones((3 * tq, HEAD_DIM), MXU_DTYPE)], axis=1))

    gate = gate_ref[...]
    for r in range(r_heads):
        rows = slice(r * tq, (r + 1) * tq)
        g_cmp = gate[:, 3 * r:3 * r + 1]
        g_win = gate[:, 3 * r + 2:3 * r + 3]
        o_w = res_w[rows, :HEAD_DIM] / res_w[rows, HEAD_DIM:]
        ocw_ref[:, r * HEAD_DIM:(r + 1) * HEAD_DIM] = g_cmp * o_c[r] + g_win * o_w


def _overlap_matrix(n_cmp_rows, n_slc):
    cs = jnp.arange(n_cmp_rows)[:, None] * CMP_STRIDE
    ss = jnp.arange(n_slc)[None, :] * SEL_BLOCK
    ov = jnp.clip(jnp.minimum(cs + CMP_BLOCK, ss + SEL_BLOCK) - jnp.maximum(cs, ss), 0, None)
    return (ov.astype(F32) / CMP_BLOCK).astype(MXU_DTYPE)


def _cmp_win(qkv, kvc, gates, s):
    tq = min(s, 256)
    assert WINDOW <= 2 * tq
    g = N_KV_GROUPS
    n_cmp = kvc.shape[1]
    n_slc = s // SEL_BLOCK
    head = lambda base: [
        pl.BlockSpec((1, tq, HEAD_DIM), functools.partial(lambda a, i, d, base: (base + a, jnp.maximum(i - d, 0), 0), d=d, base=base))
        for d in (2, 1, 0)
    ]
    ocw, imp = pl.pallas_call(
        functools.partial(_cmp_win_kernel, tq=tq, chunk=min(n_cmp, 256)),
        grid=(g, s // tq),
        in_specs=[
            pl.BlockSpec((HEADS_PER_GROUP, tq, HEAD_DIM), lambda a, i: (a, i, 0)),
            pl.BlockSpec((1, n_cmp, HEAD_DIM), lambda a, i: (a, 0, 0)),
            pl.BlockSpec((1, n_cmp, HEAD_DIM), lambda a, i: (g + a, 0, 0)),
            *head(N_HEADS + 2 * g),
            *head(N_HEADS + 3 * g),
            pl.BlockSpec((n_cmp, n_slc), lambda a, i: (0, 0)),
            pl.BlockSpec((tq, HEAD_DIM), lambda a, i: (i, a)),
        ],
        out_specs=[
            pl.BlockSpec((tq, HEADS_PER_GROUP * HEAD_DIM), lambda a, i: (i, a)),
            pl.BlockSpec((1, tq, n_slc), lambda a, i: (a, i, 0)),
        ],
        out_shape=[
            jax.ShapeDtypeStruct((s, N_HEADS * HEAD_DIM), F32),
            jax.ShapeDtypeStruct((g, s, n_slc), F32),
        ],
        compiler_params=_params("parallel", "parallel"),
        name="cmp_win",
    )(qkv, kvc, kvc, qkv, qkv, qkv, qkv, qkv, qkv, _overlap_matrix(n_cmp, n_slc), gates)
    return ocw, imp


def _topk_kernel(imp_ref, sb_ref, work_ref):
    tm, n_slc = work_ref.shape
    tok = pl.program_id(1) * tm + lax.broadcasted_iota(jnp.int32, (tm, 1), 0)
    blk = lax.broadcasted_iota(jnp.int32, (1, n_slc), 1)
    cur = tok >> SEL_SHIFT
    forced = (blk == cur) | (blk == 0)
    work_ref[...] = jnp.where(forced, REMOVED, jnp.where(blk <= cur, imp_ref[0], NEG))
    blk_f = blk.astype(F32)
    for _ in range(min(SEL_TOP, n_slc) - 2):
        w = work_ref[...]
        best = jnp.max(w, axis=-1, keepdims=True)
        idx = jnp.min(jnp.where(w == best, blk_f, float(n_slc)), axis=-1, keepdims=True)
        work_ref[...] = jnp.where(blk_f == idx, REMOVED, w)
    sb_ref[0] = jnp.where(work_ref[...] == REMOVED, 0.0, NEG).astype(sb_ref.dtype)


def _topk_bias(imp):
    g, s, n_slc = imp.shape
    tm = min(s, 1024)
    spec = pl.BlockSpec((1, tm, n_slc), lambda a, i: (a, i, 0))
    return pl.pallas_call(
        _topk_kernel,
        grid=(g, s // tm),
        in_specs=[spec],
        out_specs=spec,
        out_shape=jax.ShapeDtypeStruct((g, s, n_slc), MXU_DTYPE),
        scratch_shapes=[pltpu.VMEM((tm, n_slc), F32)],
        compiler_params=_params("parallel", "parallel"),
        name="topk_bias",
    )(imp)


def _sel_kernel(q_ref, k_ref, v_ref, sb_ref, ocw_ref, gate_ref, o_ref, lhs_ref, s_ref, m_ref, acc_ref, *, tq, tk):
    i = pl.program_id(1)
    r_heads = HEADS_PER_GROUP
    bias_lanes = lhs_ref.shape[2] - HEAD_DIM
    blocks_per_tile = tk // SEL_BLOCK
    lane_tiles = tk // HEAD_DIM
    n_tiles = (i * tq + tq - 1) // tk + 1
    tok = i * tq + lax.broadcasted_iota(jnp.int32, (tq, 1), 0)
    key_blk = lax.broadcasted_iota(jnp.int32, (tk, bias_lanes), 0) >> SEL_SHIFT
    lane = lax.broadcasted_iota(jnp.int32, (tk, bias_lanes), 1)
    key_off = lax.broadcasted_iota(jnp.int32, (1, tk), 1)

    for c in range(lhs_ref.shape[0]):
        for r in range(r_heads):
            rows = slice(r * tq, (r + 1) * tq)
            lhs_ref[c, rows, :HEAD_DIM] = q_ref[r]
            lhs_ref[c, rows, HEAD_DIM:] = sb_ref[0, :, c * bias_lanes:(c + 1) * bias_lanes]

    def produce(slot, j):
        start = pl.multiple_of(j * tk, tk)
        first_blk = j * blocks_per_tile
        onehot = jnp.where(lane == first_blk % bias_lanes + key_blk, 1.0, 0.0).astype(MXU_DTYPE)
        k_ext = jnp.concatenate([k_ref[0, pl.ds(start, tk), :], onehot], axis=1)
        s_ref[slot] = _dot_nt(lhs_ref[first_blk // bias_lanes], k_ext)

    def consume(slot, j, diagonal):
        start = pl.multiple_of(j * tk, tk)
        v_ext = jnp.concatenate([v_ref[0, pl.ds(start, tk), :], jnp.ones((tk, HEAD_DIM), MXU_DTYPE)], axis=1)
        p_rows, alphas = [], []
        for r in range(r_heads):
            rows = slice(r * tq, (r + 1) * tq)
            sm = s_ref[slot, rows, :]
            if diagonal:
                sm = jnp.where(j * tk + key_off <= tok, sm, NEG)
            m_prev = m_ref[rows, :]
            m_new = jnp.maximum(m_prev, jnp.max(sm, axis=-1, keepdims=True))
            m_ref[rows, :] = m_new
            p_rows.append(jnp.exp2(EXP2_SCALE * (sm - pltpu.repeat(m_new, lane_tiles, axis=1))).astype(MXU_DTYPE))
            alphas.append(jnp.exp2(EXP2_SCALE * (m_prev - m_new)))
        pv = _dot(jnp.concatenate(p_rows, axis=0), v_ext)
        for r in range(r_heads):
            rows = slice(r * tq, (r + 1) * tq)
            acc_ref[rows, :] = acc_ref[rows, :] * pltpu.repeat(alphas[r], 2, axis=1) + pv[rows, :]

    m_ref[...] = jnp.full(m_ref.shape, NEG, F32)
    acc_ref[...] = jnp.zeros(acc_ref.shape, F32)
    produce(0, 0)

    def body(pair, carry):
        j = 2 * pair
        consume(0, j, diagonal=False)
        produce(1, j + 1)
        consume(1, j + 1, diagonal=False)
        produce(0, j + 2)
        return carry

    n_pairs = (n_tiles - 1) // 2
    lax.fori_loop(0, n_pairs, body, 0)
    done = 2 * n_pairs

    @pl.when(n_tiles - done == 1)
    def _():
        consume(0, done, diagonal=True)

    @pl.when(n_tiles - done == 2)
    def _():
        consume(0, done, diagonal=False)
        produce(1, done + 1)
        consume(1, done + 1, diagonal=True)

    gate = gate_ref[...]
    for r in range(r_heads):
        rows = slice(r * tq, (r + 1) * tq)
        cols = slice(r * HEAD_DIM, (r + 1) * HEAD_DIM)
        o_s = acc_ref[rows, :HEAD_DIM] / acc_ref[rows, HEAD_DIM:]
        o_ref[:, cols] = (ocw_ref[:, cols] + gate[:, 3 * r + 1:3 * r + 2] * o_s).astype(o_ref.dtype)


def _sel_attention(qkv, sel_bias, ocw, gates, s):
    tq = min(s, 512)
    tk = min(s, 512)
    g = N_KV_GROUPS
    n_slc = s // SEL_BLOCK
    width = HEADS_PER_GROUP * HEAD_DIM
    rows = HEADS_PER_GROUP * tq
    bias_lanes = min(HEAD_DIM, n_slc)
    assert n_slc % bias_lanes == 0 and bias_lanes % (tk // SEL_BLOCK) == 0 and tk % tq == 0
    return pl.pallas_call(
        functools.partial(_sel_kernel, tq=tq, tk=tk),
        grid=(g, s // tq),
        in_specs=[
            pl.BlockSpec((HEADS_PER_GROUP, tq, HEAD_DIM), lambda a, i: (a, i, 0)),
            pl.BlockSpec((1, s, HEAD_DIM), lambda a, i: (N_HEADS + a, 0, 0)),
            pl.BlockSpec((1, s, HEAD_DIM), lambda a, i: (N_HEADS + g + a, 0, 0)),
            pl.BlockSpec((1, tq, n_slc), lambda a, i: (a, i, 0)),
            pl.BlockSpec((tq, width), lambda a, i: (i, a)),
            pl.BlockSpec((tq, HEAD_DIM), lambda a, i: (i, a)),
        ],
        out_specs=pl.BlockSpec((tq, width), lambda a, i: (i, a)),
        out_shape=jax.ShapeDtypeStruct((s, N_HEADS * HEAD_DIM), MXU_DTYPE),
        scratch_shapes=[
            pltpu.VMEM((n_slc // bias_lanes, rows, HEAD_DIM + bias_lanes), MXU_DTYPE),
            pltpu.VMEM((2, rows, tk), F32),
            pltpu.VMEM((rows, HEAD_DIM), F32),
            pltpu.VMEM((rows, 2 * HEAD_DIM), F32),
        ],
        compiler_params=_params("parallel", "arbitrary"),
        name="sel_attention",
    )(qkv, qkv, qkv, sel_bias, ocw, gates)


def _merge_kernel(h_ref, cf_ref, at_ref, wga_ref, wgb_ref, wco_ref, wno_ref, o_ref):
    h = h_ref[...]
    y_a = _dot(cf_ref[...], wco_ref[...])
    y_b = _dot(at_ref[...], wno_ref[...])
    merged = jax.nn.sigmoid(_dot(h, wga_ref[...])) * y_a + jax.nn.sigmoid(_dot(h, wgb_ref[...])) * y_b
    o_ref[...] = merged.astype(o_ref.dtype)


def _merge(h, cf, attn, w_ga, w_gb, w_co, w_no):
    s, d = h.shape
    tm = min(s, 1024)
    tn = 512
    row = lambda width: pl.BlockSpec((tm, width), lambda i, j: (i, 0))
    col = lambda depth: pl.BlockSpec((depth, tn), lambda i, j: (0, j))
    return pl.pallas_call(
        _merge_kernel,
        grid=(s // tm, d // tn),
        in_specs=[row(d), row(CONV_WIDTH), row(N_HEADS * HEAD_DIM), col(d), col(d), col(CONV_WIDTH), col(N_HEADS * HEAD_DIM)],
        out_specs=pl.BlockSpec((tm, tn), lambda i, j: (i, j)),
        out_shape=jax.ShapeDtypeStruct((s, d), MXU_DTYPE),
        compiler_params=_params("parallel", "parallel"),
        name="merge",
    )(h, cf, attn, w_ga, w_gb, w_co, w_no)


def _out_proj_kernel(m_ref, w_ref, x_ref, gain_ref, g_ref, o_ref):
    y = _dot(m_ref[...], w_ref[...])
    o_ref[...] = x_ref[...] + g_ref[...] * _rms(y, gain_ref[...])


def _out_proj(merged, w_out, x, gain, g1):
    s, d = x.shape
    tm = min(s, 512)
    vec = pl.BlockSpec((1, d), lambda i: (0, 0))
    row = pl.BlockSpec((tm, d), lambda i: (i, 0))
    return pl.pallas_call(
        _out_proj_kernel,
        grid=(s // tm,),
        in_specs=[row, pl.BlockSpec((d, d), lambda i: (0, 0)), row, vec, vec],
        out_specs=row,
        out_shape=jax.ShapeDtypeStruct((s, d), F32),
        compiler_params=_params("parallel"),
        name="out_proj",
    )(merged, w_out, x, gain, g1)


def _mlp_kernel(x_ref, gin_ref, sc_ref, sh_ref, wu_ref, wd_ref, gout_ref, g_ref, o_ref, h_ref, acc_ref):
    f = pl.program_id(1)

    @pl.when(f == 0)
    def _():
        y = _rms(x_ref[...], gin_ref[...])
        h_ref[...] = (y * (1.0 + sc_ref[...]) + sh_ref[...]).astype(h_ref.dtype)
        acc_ref[...] = jnp.zeros(acc_ref.shape, F32)

    u = jnp.square(jnp.maximum(_dot(h_ref[...], wu_ref[...]), 0.0))
    acc_ref[...] += _dot(u.astype(MXU_DTYPE), wd_ref[...])

    @pl.when(f == pl.num_programs(1) - 1)
    def _():
        o_ref[...] = x_ref[...] + g_ref[...] * _rms(acc_ref[...], gout_ref[...])


def _mlp(x, gain_in, sc, sh, w_up, w_down, gain_out, g2):
    s, d = x.shape
    tm = min(s, 512)
    tf = 1024
    vec = pl.BlockSpec((1, d), lambda i, f: (0, 0))
    row = pl.BlockSpec((tm, d), lambda i, f: (i, 0))
    return pl.pallas_call(
        _mlp_kernel,
        grid=(s // tm, D_FF // tf),
        in_specs=[row, vec, vec, vec, pl.BlockSpec((d, tf), lambda i, f: (0, f)), pl.BlockSpec((tf, d), lambda i, f: (f, 0)), vec, vec],
        out_specs=row,
        out_shape=jax.ShapeDtypeStruct((s, d), F32),
        scratch_shapes=[pltpu.VMEM((tm, d), MXU_DTYPE), pltpu.VMEM((tm, d), F32)],
        compiler_params=_params("parallel", "arbitrary"),
        name="mlp",
    )(x, gain_in, sc, sh, w_up, w_down, gain_out, g2)


def _gate_weight(w_in_l):
    w = w_in_l[:, COL_GATE:COL_GA].reshape(D_MODEL, N_KV_GROUPS, GATES_PER_GROUP)
    w = jnp.pad(w, ((0, 0), (0, 0), (0, HEAD_DIM - GATES_PER_GROUP)))
    return w.reshape(D_MODEL, N_KV_GROUPS * HEAD_DIM).astype(MXU_DTYPE)


def kernel(x, c, positions, ada_w, ada_b, norm_gains, w_in, conv_w, w_conv_out, cmp_pe, cmp_w1, cmp_b1, cmp_w2, cmp_b2,
           w_nsa_out, w_out, w_mlp_up, w_mlp_down):
    b, s, d = x.shape
    assert b == 1 and d == D_MODEL
    depth = ada_w.shape[0]
    cast = lambda w: w.astype(MXU_DTYPE)
    xs = x.reshape(s, d)
    mod = _adaln(c, ada_w, ada_b)
    tables = _rope_tables(positions)
    ks_col = COL_KV + 2 * KV_WIDTH
    for l in range(depth):
        sh1, sc1, g1, sh2, sc2, g2 = [mod[l, k * d:(k + 1) * d].reshape(1, d) for k in range(6)]
        gains = [norm_gains[l, k].reshape(1, d) for k in range(4)]
        wl = w_in[l]
        h = _norm_mod(xs, gains[0], sc1, sh1)
        cf = _conv_proj(h, cast(wl[:, :COL_Q]), conv_w[l])
        w_qkv = cast(jnp.concatenate([wl[:, COL_Q:COL_KV], wl[:, ks_col:COL_GATE]], axis=1))
        qkv = _head_proj(h, w_qkv, tables, (0, 1, 2, 3, 4, 6), MXU_DTYPE)
        kv_raw = _head_proj(h, cast(wl[:, COL_KV:ks_col]), tables, (0,), F32)
        gates = _gate_proj(h, _gate_weight(wl))
        kvc = _compress(kv_raw, cmp_pe[l], cast(cmp_w1[l]), cmp_b1[l], cast(cmp_w2[l]), cmp_b2[l])
        ocw, imp = _cmp_win(qkv, kvc, gates, s)
        attn = _sel_attention(qkv, _topk_bias(imp), ocw, gates, s)
        merged = _merge(h, cf, attn, cast(wl[:, COL_GA:COL_GB]), cast(wl[:, COL_GB:]), cast(w_conv_out[l]), cast(w_nsa_out[l]))
        xs = _out_proj(merged, cast(w_out[l]), xs, gains[1], g1)
        xs = _mlp(xs, gains[2], sc2, sh2, cast(w_mlp_up[l]), cast(w_mlp_down[l]), gains[3], g2)
    return xs.reshape(b, s, d)
```

```python
import functools

import jax
import jax.numpy as jnp
from jax import lax
from jax.experimental import pallas as pl
from jax.experimental.pallas import tpu as pltpu

F32 = jnp.float32
MXU_DTYPE = jnp.bfloat16

D_MODEL = 2048
CONV_WIDTH = D_MODEL // 2
N_HEADS = 16
N_KV_GROUPS = 4
HEADS_PER_GROUP = N_HEADS // N_KV_GROUPS
HEAD_DIM = D_MODEL // N_HEADS
KV_WIDTH = N_KV_GROUPS * HEAD_DIM
ROPE_DIM = HEAD_DIM // 4
ROPE_THETA = 500000.0
CMP_BLOCK = 32
CMP_STRIDE = 16
CMP_HIDDEN = 2 * HEAD_DIM
SEL_BLOCK = 64
SEL_SHIFT = 6
SEL_TOP = 16
WINDOW = 512
D_FF = 4 * D_MODEL
NORM_EPS = 1e-6
NEG = -1e30
REMOVED = -3e38
EXP2_SCALE = HEAD_DIM ** -0.5 * 1.4426950408889634

COL_Q = 3 * CONV_WIDTH
COL_KV = COL_Q + N_HEADS * HEAD_DIM
COL_GATE = COL_KV + 6 * KV_WIDTH
COL_GA = COL_GATE + 3 * N_HEADS
COL_GB = COL_GA + D_MODEL
GATES_PER_GROUP = 3 * HEADS_PER_GROUP

VMEM_LIMIT_BYTES = 56 * 1024 * 1024


def _params(*semantics):
    return pltpu.CompilerParams(dimension_semantics=semantics, vmem_limit_bytes=VMEM_LIMIT_BYTES)


def _dot(a, b):
    return jnp.dot(a, b, preferred_element_type=F32)


def _dot_nt(a, b):
    return lax.dot_general(a, b, (((1,), (1,)), ((), ())), preferred_element_type=F32)


def _rms(y, gain):
    ms = jnp.mean(y * y, axis=-1, keepdims=True)
    return y * lax.rsqrt(ms + NORM_EPS) * gain


def _adaln_kernel(c_ref, w_ref, b_ref, o_ref):
    c = c_ref[...]
    act = (c * jax.nn.sigmoid(c)).astype(MXU_DTYPE)
    o_ref[0] = _dot(act, w_ref[0].astype(MXU_DTYPE)) + b_ref[0]


def _adaln(c, ada_w, ada_b):
    depth, d, n = ada_w.shape
    tn = 1024
    c8 = jnp.broadcast_to(c, (8, d))
    out = pl.pallas_call(
        _adaln_kernel,
        grid=(depth, n // tn),
        in_specs=[
            pl.BlockSpec((8, d), lambda l, j: (0, 0)),
            pl.BlockSpec((1, d, tn), lambda l, j: (l, 0, j)),
            pl.BlockSpec((1, 1, tn), lambda l, j: (l, 0, j)),
        ],
        out_specs=pl.BlockSpec((1, 8, tn), lambda l, j: (l, 0, j)),
        out_shape=jax.ShapeDtypeStruct((depth, 8, n), F32),
        compiler_params=_params("parallel", "parallel"),
        name="adaln",
    )(c8, ada_w, ada_b.reshape(depth, 1, n))
    return out[:, 0, :]


def _rope_table_kernel(pos_ref, freq_ref, c_ref, s1_ref, s2_ref):
    half = ROPE_DIM // 2
    ang = pos_ref[...].astype(F32) * freq_ref[...]
    lane = lax.broadcasted_iota(jnp.int32, ang.shape, 1)
    cos = jnp.cos(ang)
    sin = jnp.sin(ang)
    c_ref[...] = jnp.where(lane < ROPE_DIM, cos, 1.0)
    s1_ref[...] = jnp.where(lane < half, -sin, 0.0)
    s2_ref[...] = jnp.where((lane >= half) & (lane < ROPE_DIM), sin, 0.0)


def _rope_tables(positions):
    s = positions.shape[-1]
    tm = min(s, 2048)
    half = ROPE_DIM // 2
    inv_freq = ROPE_THETA ** (-jnp.arange(0, ROPE_DIM, 2, dtype=F32) / ROPE_DIM)
    freq = jnp.concatenate([inv_freq, inv_freq, jnp.zeros((HEAD_DIM - 2 * half,), F32)]).reshape(1, HEAD_DIM)
    spec = pl.BlockSpec((tm, HEAD_DIM), lambda i: (i, 0))
    shape = jax.ShapeDtypeStruct((s, HEAD_DIM), F32)
    return pl.pallas_call(
        _rope_table_kernel,
        grid=(s // tm,),
        in_specs=[pl.BlockSpec((tm, 1), lambda i: (i, 0)), pl.BlockSpec((1, HEAD_DIM), lambda i: (0, 0))],
        out_specs=[spec, spec, spec],
        out_shape=[shape, shape, shape],
        compiler_params=_params("parallel"),
        name="rope_tables",
    )(positions.reshape(s, 1), freq)


def _rope(x, c, s1, s2):
    half = ROPE_DIM // 2
    return x * c + pltpu.roll(x, HEAD_DIM - half, axis=1) * s1 + pltpu.roll(x, half, axis=1) * s2


def _norm_mod_kernel(x_ref, g_ref, sc_ref, sh_ref, o_ref):
    y = _rms(x_ref[...], g_ref[...])
    o_ref[...] = (y * (1.0 + sc_ref[...]) + sh_ref[...]).astype(o_ref.dtype)


def _norm_mod(x, gain, sc, sh):
    s, d = x.shape
    tm = min(s, 512)
    vec = pl.BlockSpec((1, d), lambda i: (0, 0))
    return pl.pallas_call(
        _norm_mod_kernel,
        grid=(s // tm,),
        in_specs=[pl.BlockSpec((tm, d), lambda i: (i, 0)), vec, vec, vec],
        out_specs=pl.BlockSpec((tm, d), lambda i: (i, 0)),
        out_shape=jax.ShapeDtypeStruct((s, d), MXU_DTYPE),
        compiler_params=_params("parallel"),
        name="norm_mod",
    )(x, gain, sc, sh)


def _conv_proj_kernel(h_ref, wb_ref, wc_ref, wx_ref, cw_ref, o_ref, carry_ref):
    i = pl.program_id(0)
    j = pl.program_id(1)
    tm = h_ref.shape[0]
    h = h_ref[...]
    u = _dot(h, wc_ref[...]) * _dot(h, wx_ref[...])

    @pl.when(i == 0)
    def _():
        carry_ref[j] = jnp.zeros(carry_ref.shape[1:], F32)

    prev = carry_ref[j]
    carry_ref[j] = u[tm - 8:, :]
    row = lax.broadcasted_iota(jnp.int32, u.shape, 0)
    u1 = jnp.where(row == 0, prev[7:8, :], pltpu.roll(u, 1, axis=0))
    u2 = jnp.where(row == 0, prev[6:7, :], jnp.where(row == 1, prev[7:8, :], pltpu.roll(u, 2, axis=0)))
    cw = cw_ref[...]
    z = cw[2:3, :] * u + cw[1:2, :] * u1 + cw[0:1, :] * u2
    o_ref[...] = (_dot(h, wb_ref[...]) * z).astype(o_ref.dtype)


def _conv_proj(h, w_conv, conv_w):
    s, d = h.shape
    tm = min(s, 1024)
    tn = 512
    nb = CONV_WIDTH // tn
    return pl.pallas_call(
        _conv_proj_kernel,
        grid=(s // tm, nb),
        in_specs=[
            pl.BlockSpec((tm, d), lambda i, j: (i, 0)),
            pl.BlockSpec((d, tn), lambda i, j: (0, j)),
            pl.BlockSpec((d, tn), lambda i, j: (0, j + nb)),
            pl.BlockSpec((d, tn), lambda i, j: (0, j + 2 * nb)),
            pl.BlockSpec((3, tn), lambda i, j: (0, j)),
        ],
        out_specs=pl.BlockSpec((tm, tn), lambda i, j: (i, j)),
        out_shape=jax.ShapeDtypeStruct((s, CONV_WIDTH), MXU_DTYPE),
        scratch_shapes=[pltpu.VMEM((nb, 8, tn), F32)],
        compiler_params=_params("arbitrary", "arbitrary"),
        name="conv_proj",
    )(h, w_conv, w_conv, w_conv, conv_w)


def _head_proj_kernel(h_ref, w_ref, c_ref, s1_ref, s2_ref, o_ref, *, rope_groups, query_groups):
    j = pl.program_id(1)
    groups_per_step = o_ref.shape[0] // HEADS_PER_GROUP
    width = HEADS_PER_GROUP * HEAD_DIM
    any_of = lambda group, members: functools.reduce(jnp.logical_or, [group == r for r in members], False)
    for k in range(groups_per_step):
        acc = _dot(h_ref[...], w_ref[:, k * width:(k + 1) * width])
        is_rope = any_of(j * groups_per_step + k, rope_groups)
        q_scale = jnp.where(any_of(j * groups_per_step + k, query_groups), EXP2_SCALE, 1.0)
        c = jnp.where(is_rope, c_ref[...], 1.0) * q_scale
        s1 = jnp.where(is_rope, s1_ref[...], 0.0) * q_scale
        s2 = jnp.where(is_rope, s2_ref[...], 0.0) * q_scale
        for r in range(HEADS_PER_GROUP):
            head = _rope(acc[:, r * HEAD_DIM:(r + 1) * HEAD_DIM], c, s1, s2)
            o_ref[k * HEADS_PER_GROUP + r] = head.astype(o_ref.dtype)


def _head_proj(h, w, tables, rope_groups, query_groups, out_dtype):
    s, d = h.shape
    tm = min(s, 512)
    tn = min(w.shape[1], 4 * HEADS_PER_GROUP * HEAD_DIM)
    heads_per_step = tn // HEAD_DIM
    tab = pl.BlockSpec((tm, HEAD_DIM), lambda i, j: (i, 0))
    return pl.pallas_call(
        functools.partial(_head_proj_kernel, rope_groups=rope_groups, query_groups=query_groups),
        grid=(s // tm, w.shape[1] // tn),
        in_specs=[pl.BlockSpec((tm, d), lambda i, j: (i, 0)), pl.BlockSpec((d, tn), lambda i, j: (0, j)), tab, tab, tab],
        out_specs=pl.BlockSpec((heads_per_step, tm, HEAD_DIM), lambda i, j: (j, i, 0)),
        out_shape=jax.ShapeDtypeStruct((w.shape[1] // HEAD_DIM, s, HEAD_DIM), out_dtype),
        compiler_params=_params("parallel", "parallel"),
        name="head_proj",
    )(h, w, *tables)


def _gate_proj_kernel(h_ref, w_ref, o_ref):
    o_ref[...] = jax.nn.sigmoid(_dot(h_ref[...], w_ref[...]))


def _gate_proj(h, w_gate):
    s, d = h.shape
    tm = min(s, 1024)
    n = w_gate.shape[1]
    return pl.pallas_call(
        _gate_proj_kernel,
        grid=(s // tm,),
        in_specs=[pl.BlockSpec((tm, d), lambda i: (i, 0)), pl.BlockSpec((d, n), lambda i: (0, 0))],
        out_specs=pl.BlockSpec((tm, n), lambda i: (i, 0)),
        out_shape=jax.ShapeDtypeStruct((s, n), F32),
        compiler_params=_params("parallel"),
        name="gate_proj",
    )(h, w_gate)


def _compress_kernel(x_ref, pe_ref, w1_ref, b1_ref, w2_ref, b2_ref, o_ref):
    n = o_ref.shape[1]
    k_lo = CMP_STRIDE * HEAD_DIM
    pe = pe_ref[0]
    lo, hi = [], []
    for l in range(CMP_STRIDE):
        rows = x_ref[0, pl.ds(l, n, stride=CMP_STRIDE), :]
        lo.append((rows + pe[l:l + 1, :]).astype(MXU_DTYPE))
        hi.append((rows + pe[CMP_STRIDE + l:CMP_STRIDE + l + 1, :]).astype(MXU_DTYPE))
    t_lo = _dot(jnp.concatenate(lo, axis=1), w1_ref[0, :k_lo, :])
    t_hi = _dot(jnp.concatenate(hi, axis=1), w1_ref[0, k_lo:, :])
    hid = t_lo + pltpu.roll(t_hi, n - 1, axis=0) + b1_ref[0]
    out = _dot(jax.nn.gelu(hid).astype(MXU_DTYPE), w2_ref[0]) + b2_ref[0]
    row = lax.broadcasted_iota(jnp.int32, out.shape, 0)
    o_ref[0] = jnp.where(row == n - 1, 0.0, out).astype(o_ref.dtype)


def _compress(kv_raw, pe, w1, b1, w2, b2):
    nh, s, dh = kv_raw.shape
    n = s // CMP_STRIDE
    g = N_KV_GROUPS
    return pl.pallas_call(
        _compress_kernel,
        grid=(nh,),
        in_specs=[
            pl.BlockSpec((1, s, dh), lambda a: (a, 0, 0)),
            pl.BlockSpec((1, CMP_BLOCK, dh), lambda a: (a // g, 0, 0)),
            pl.BlockSpec((1, CMP_BLOCK * dh, CMP_HIDDEN), lambda a: (a // g, 0, 0)),
            pl.BlockSpec((1, 1, CMP_HIDDEN), lambda a: (a // g, 0, 0)),
            pl.BlockSpec((1, CMP_HIDDEN, dh), lambda a: (a // g, 0, 0)),
            pl.BlockSpec((1, 1, dh), lambda a: (a // g, 0, 0)),
        ],
        out_specs=pl.BlockSpec((1, n, dh), lambda a: (a, 0, 0)),
        out_shape=jax.ShapeDtypeStruct((nh, n, dh), MXU_DTYPE),
        compiler_params=_params("parallel"),
        name="compress",
    )(kv_raw, pe, w1, b1.reshape(2, 1, CMP_HIDDEN), w2, b2.reshape(2, 1, dh))


def _softmax_numerators(s, bias):
    sm = s + bias
    m = jnp.max(sm, axis=-1, keepdims=True)
    m = jnp.where(m < 0.5 * NEG, 0.0, m)
    return jnp.exp2(sm - m).astype(MXU_DTYPE)


def _cmp_win_kernel(q_ref, kc_ref, vc_ref, kw2_ref, kw1_ref, kw0_ref, vw2_ref, vw1_ref, vw0_ref, ov_ref, gate_ref,
                    ocw_ref, imp_ref, *, tq, chunk):
    n_cmp = kc_ref.shape[1]
    last_visible = ((pl.program_id(1) + 1) * tq - CMP_BLOCK) // CMP_STRIDE
    for c in range(n_cmp // chunk):
        pl.when(jnp.maximum(last_visible, 0) // chunk == c)(functools.partial(
            _cmp_win_body, q_ref, kc_ref, vc_ref, kw2_ref, kw1_ref, kw0_ref, vw2_ref, vw1_ref, vw0_ref, ov_ref, gate_ref,
            ocw_ref, imp_ref, tq=tq, n_cmp=(c + 1) * chunk))


def _cmp_win_body(q_ref, kc_ref, vc_ref, kw2_ref, kw1_ref, kw0_ref, vw2_ref, vw1_ref, vw0_ref, ov_ref, gate_ref,
                  ocw_ref, imp_ref, *, tq, n_cmp):
    i = pl.program_id(1)
    r_heads = HEADS_PER_GROUP
    s0 = i * tq
    q4 = q_ref[...].reshape(r_heads * tq, HEAD_DIM)
    tok = s0 + lax.broadcasted_iota(jnp.int32, (tq, 1), 0)

    kwin = jnp.concatenate([kw2_ref[0], kw1_ref[0], kw0_ref[0]], axis=0)
    s_w = _dot_nt(q4, kwin)
    s_c = _dot_nt(q4, kc_ref[0, :n_cmp, :])
    vwin = jnp.concatenate([vw2_ref[0], vw1_ref[0], vw0_ref[0]], axis=0)
    wpos = s0 - 2 * tq + lax.broadcasted_iota(jnp.int32, (1, 3 * tq), 1)
    diff = tok - wpos
    bias_w = jnp.where((diff >= 0) & (diff < WINDOW) & (wpos >= 0), 0.0, NEG)
    p_w = jnp.concatenate([_softmax_numerators(s_w[r * tq:(r + 1) * tq, :], bias_w) for r in range(r_heads)], axis=0)
    res_w = _dot(p_w, jnp.concatenate([vwin, jnp.ones((3 * tq, HEAD_DIM), MXU_DTYPE)], axis=1))
    cmp_end = lax.broadcasted_iota(jnp.int32, (1, n_cmp), 1) * CMP_STRIDE + (CMP_BLOCK - 1)
    bias_c = jnp.where(cmp_end <= tok, 0.0, NEG)
    p_c = jnp.concatenate([_softmax_numerators(s_c[r * tq:(r + 1) * tq, :], bias_c) for r in range(r_heads)], axis=0)
    ones_c = jnp.ones((n_cmp, HEAD_DIM), MXU_DTYPE)
    res_c = _dot(p_c, jnp.concatenate([vc_ref[0, :n_cmp, :], ones_c, ov_ref[:n_cmp, :]], axis=1))
    o_c, imp = [], None
    for r in range(r_heads):
        rows = slice(r * tq, (r + 1) * tq)
        l = res_c[rows, HEAD_DIM:2 * HEAD_DIM]
        inv = jnp.where(l > 0.0, 1.0 / l, 0.0)
        o_c.append(res_c[rows, :HEAD_DIM] * inv)
        part = res_c[rows, 2 * HEAD_DIM:] * inv[:, :1]
        imp = part if imp is None else imp + part
    imp_ref[0] = imp

    gate = gate_ref[...]
    for r in range(r_heads):
        rows = slice(r * tq, (r + 1) * tq)
        g_cmp = gate[:, 3 * r:3 * r + 1]
        g_win = gate[:, 3 * r + 2:3 * r + 3]
        o_w = res_w[rows, :HEAD_DIM] / res_w[rows, HEAD_DIM:]
        ocw_ref[:, r * HEAD_DIM:(r + 1) * HEAD_DIM] = g_cmp * o_c[r] + g_win * o_w


def _overlap_matrix(n_cmp_rows, n_slc):
    cs = jnp.arange(n_cmp_rows)[:, None] * CMP_STRIDE
    ss = jnp.arange(n_slc)[None, :] * SEL_BLOCK
    ov = jnp.clip(jnp.minimum(cs + CMP_BLOCK, ss + SEL_BLOCK) - jnp.maximum(cs, ss), 0, None)
    return (ov.astype(F32) / CMP_BLOCK).astype(MXU_DTYPE)


def _cmp_win(qkv, kvc, gates, s):
    tq = min(s, 256)
    assert WINDOW <= 2 * tq
    g = N_KV_GROUPS
    n_cmp = kvc.shape[1]
    n_slc = s // SEL_BLOCK
    head = lambda base: [
        pl.BlockSpec((1, tq, HEAD_DIM), functools.partial(lambda a, i, d, base: (base + a, jnp.maximum(i - d, 0), 0), d=d, base=base))
        for d in (2, 1, 0)
    ]
    ocw, imp = pl.pallas_call(
        functools.partial(_cmp_win_kernel, tq=tq, chunk=min(n_cmp, 256)),
        grid=(g, s // tq),
        in_specs=[
            pl.BlockSpec((HEADS_PER_GROUP, tq, HEAD_DIM), lambda a, i: (a, i, 0)),
            pl.BlockSpec((1, n_cmp, HEAD_DIM), lambda a, i: (a, 0, 0)),
            pl.BlockSpec((1, n_cmp, HEAD_DIM), lambda a, i: (g + a, 0, 0)),
            *head(N_HEADS + 2 * g),
            *head(N_HEADS + 3 * g),
            pl.BlockSpec((n_cmp, n_slc), lambda a, i: (0, 0)),
            pl.BlockSpec((tq, HEAD_DIM), lambda a, i: (i, a)),
        ],
        out_specs=[
            pl.BlockSpec((tq, HEADS_PER_GROUP * HEAD_DIM), lambda a, i: (i, a)),
            pl.BlockSpec((1, tq, n_slc), lambda a, i: (a, i, 0)),
        ],
        out_shape=[
            jax.ShapeDtypeStruct((s, N_HEADS * HEAD_DIM), F32),
            jax.ShapeDtypeStruct((g, s, n_slc), F32),
        ],
        compiler_params=_params("parallel", "parallel"),
        name="cmp_win",
    )(qkv, kvc, kvc, qkv, qkv, qkv, qkv, qkv, qkv, _overlap_matrix(n_cmp, n_slc), gates)
    return ocw, imp


def _topk_kernel(imp_ref, sb_ref, work_ref):
    tm, n_slc = work_ref.shape
    tok = pl.program_id(1) * tm + lax.broadcasted_iota(jnp.int32, (tm, 1), 0)
    blk = lax.broadcasted_iota(jnp.int32, (1, n_slc), 1)
    cur = tok >> SEL_SHIFT
    forced = (blk == cur) | (blk == 0)
    work_ref[...] = jnp.where(forced, REMOVED, jnp.where(blk <= cur, imp_ref[0], NEG))
    blk_f = blk.astype(F32)
    for _ in range(min(SEL_TOP, n_slc) - 2):
        w = work_ref[...]
        best = jnp.max(w, axis=-1, keepdims=True)
        idx = jnp.min(jnp.where(w == best, blk_f, float(n_slc)), axis=-1, keepdims=True)
        work_ref[...] = jnp.where(blk_f == idx, REMOVED, w)
    sb_ref[0] = jnp.where(work_ref[...] == REMOVED, 0.0, NEG).astype(sb_ref.dtype)


def _topk_bias(imp):
    g, s, n_slc = imp.shape
    tm = min(s, 1024)
    spec = pl.BlockSpec((1, tm, n_slc), lambda a, i: (a, i, 0))
    return pl.pallas_call(
        _topk_kernel,
        grid=(g, s // tm),
        in_specs=[spec],
        out_specs=spec,
        out_shape=jax.ShapeDtypeStruct((g, s, n_slc), MXU_DTYPE),
        scratch_shapes=[pltpu.VMEM((tm, n_slc), F32)],
        compiler_params=_params("parallel", "parallel"),
        name="topk_bias",
    )(imp)


def _sel_kernel(q_ref, k_ref, v_ref, sb_ref, ocw_ref, gate_ref, o_ref, lhs_ref, s_ref, m_ref, acc_ref, *, tq, tk):
    i = pl.program_id(1)
    r_heads = HEADS_PER_GROUP
    bias_lanes = lhs_ref.shape[2] - HEAD_DIM
    blocks_per_tile = tk // SEL_BLOCK
    lane_tiles = tk // HEAD_DIM
    n_tiles = (i * tq + tq - 1) // tk + 1
    tok = i * tq + lax.broadcasted_iota(jnp.int32, (tq, 1), 0)
    key_blk = lax.broadcasted_iota(jnp.int32, (tk, bias_lanes), 0) >> SEL_SHIFT
    lane = lax.broadcasted_iota(jnp.int32, (tk, bias_lanes), 1)
    key_off = lax.broadcasted_iota(jnp.int32, (1, tk), 1)

    for c in range(lhs_ref.shape[0]):
        for r in range(r_heads):
            rows = slice(r * tq, (r + 1) * tq)
            lhs_ref[c, rows, :HEAD_DIM] = q_ref[r]
            lhs_ref[c, rows, HEAD_DIM:] = sb_ref[0, :, c * bias_lanes:(c + 1) * bias_lanes]

    def produce(slot, j, parts=1):
        start = pl.multiple_of(j * tk, tk)
        first_blk = j * blocks_per_tile
        onehot = jnp.where(lane == first_blk % bias_lanes + key_blk, 1.0, 0.0).astype(MXU_DTYPE)
        k_ext = jnp.concatenate([k_ref[0, pl.ds(start, tk), :], onehot], axis=1)
        n = r_heads * tq // parts
        for h in range(parts):
            s_ref[slot, h * n:(h + 1) * n, :] = _dot_nt(lhs_ref[first_blk // bias_lanes, h * n:(h + 1) * n, :], k_ext)

    def consume(slot, j, diagonal, parts=1):
        start = pl.multiple_of(j * tk, tk)
        v_ext = jnp.concatenate([v_ref[0, pl.ds(start, tk), :], jnp.ones((tk, HEAD_DIM), MXU_DTYPE)], axis=1)
        p_rows, alphas = [], []
        for r in range(r_heads):
            rows = slice(r * tq, (r + 1) * tq)
            sm = s_ref[slot, rows, :]
            if diagonal:
                sm = jnp.where(j * tk + key_off <= tok, sm, NEG)
            m_prev = m_ref[rows, :]
            m_new = jnp.maximum(m_prev, jnp.max(sm, axis=-1, keepdims=True))
            m_ref[rows, :] = m_new
            p_rows.append(jnp.exp2(sm - jnp.concatenate([m_new] * lane_tiles, axis=1)).astype(MXU_DTYPE))
            alphas.append(jnp.exp2(m_prev - m_new))
        per = r_heads // parts
        for h in range(parts):
            pv = _dot(jnp.concatenate(p_rows[h * per:(h + 1) * per], axis=0), v_ext)
            for r in range(per):
                rows = slice((h * per + r) * tq, (h * per + r + 1) * tq)
                scale = jnp.concatenate([alphas[h * per + r]] * 2, axis=1)
                acc_ref[rows, :] = acc_ref[rows, :] * scale + pv[r * tq:(r + 1) * tq, :]

    m_ref[...] = jnp.full(m_ref.shape, NEG, F32)
    acc_ref[...] = jnp.zeros(acc_ref.shape, F32)
    produce(0, 0, parts=2)

    def body(pair, carry):
        j = 2 * pair
        consume(0, j, diagonal=False)
        produce(1, j + 1)
        consume(1, j + 1, diagonal=False)
        produce(0, j + 2)
        return carry

    n_pairs = (n_tiles - 1) // 2
    lax.fori_loop(0, n_pairs, body, 0)
    done = 2 * n_pairs

    @pl.when(n_tiles - done == 1)
    def _():
        consume(0, done, diagonal=True, parts=2)

    @pl.when(n_tiles - done == 2)
    def _():
        consume(0, done, diagonal=False)
        produce(1, done + 1)
        consume(1, done + 1, diagonal=True, parts=2)

    gate = gate_ref[...]
    for r in range(r_heads):
        rows = slice(r * tq, (r + 1) * tq)
        cols = slice(r * HEAD_DIM, (r + 1) * HEAD_DIM)
        o_s = acc_ref[rows, :HEAD_DIM] / acc_ref[rows, HEAD_DIM:]
        o_ref[:, cols] = (ocw_ref[:, cols] + gate[:, 3 * r + 1:3 * r + 2] * o_s).astype(o_ref.dtype)


def _sel_attention(qkv, sel_bias, ocw, gates, s):
    tq = min(s, 512)
    tk = min(s, 512)
    g = N_KV_GROUPS
    n_slc = s // SEL_BLOCK
    width = HEADS_PER_GROUP * HEAD_DIM
    rows = HEADS_PER_GROUP * tq
    bias_lanes = min(HEAD_DIM, n_slc)
    assert n_slc % bias_lanes == 0 and bias_lanes % (tk // SEL_BLOCK) == 0 and tk % tq == 0
    return pl.pallas_call(
        functools.partial(_sel_kernel, tq=tq, tk=tk),
        grid=(g, s // tq),
        in_specs=[
            pl.BlockSpec((HEADS_PER_GROUP, tq, HEAD_DIM), lambda a, i: (a, i, 0)),
            pl.BlockSpec((1, s, HEAD_DIM), lambda a, i: (N_HEADS + a, 0, 0)),
            pl.BlockSpec((1, s, HEAD_DIM), lambda a, i: (N_HEADS + g + a, 0, 0)),
            pl.BlockSpec((1, tq, n_slc), lambda a, i: (a, i, 0)),
            pl.BlockSpec((tq, width), lambda a, i: (i, a)),
            pl.BlockSpec((tq, HEAD_DIM), lambda a, i: (i, a)),
        ],
        out_specs=pl.BlockSpec((tq, width), lambda a, i: (i, a)),
        out_shape=jax.ShapeDtypeStruct((s, N_HEADS * HEAD_DIM), MXU_DTYPE),
        scratch_shapes=[
            pltpu.VMEM((n_slc // bias_lanes, rows, HEAD_DIM + bias_lanes), MXU_DTYPE),
            pltpu.VMEM((2, rows, tk), F32),
            pltpu.VMEM((rows, HEAD_DIM), F32),
            pltpu.VMEM((rows, 2 * HEAD_DIM), F32),
        ],
        compiler_params=_params("parallel", "arbitrary"),
        name="sel_attention",
    )(qkv, qkv, qkv, sel_bias, ocw, gates)


def _merge_kernel(h_ref, cf_ref, at_ref, wga_ref, wgb_ref, wco_ref, wno_ref, o_ref):
    h = h_ref[...]
    y_a = _dot(cf_ref[...], wco_ref[...])
    y_b = _dot(at_ref[...], wno_ref[...])
    merged = jax.nn.sigmoid(_dot(h, wga_ref[...])) * y_a + jax.nn.sigmoid(_dot(h, wgb_ref[...])) * y_b
    o_ref[...] = merged.astype(o_ref.dtype)


def _merge(h, cf, attn, w_ga, w_gb, w_co, w_no):
    s, d = h.shape
    tm = min(s, 1024)
    tn = 512
    row = lambda width: pl.BlockSpec((tm, width), lambda i, j: (i, 0))
    col = lambda depth: pl.BlockSpec((depth, tn), lambda i, j: (0, j))
    return pl.pallas_call(
        _merge_kernel,
        grid=(s // tm, d // tn),
        in_specs=[row(d), row(CONV_WIDTH), row(N_HEADS * HEAD_DIM), col(d), col(d), col(CONV_WIDTH), col(N_HEADS * HEAD_DIM)],
        out_specs=pl.BlockSpec((tm, tn), lambda i, j: (i, j)),
        out_shape=jax.ShapeDtypeStruct((s, d), MXU_DTYPE),
        compiler_params=_params("parallel", "parallel"),
        name="merge",
    )(h, cf, attn, w_ga, w_gb, w_co, w_no)


def _out_proj_kernel(m_ref, w_ref, x_ref, gain_ref, g_ref, o_ref):
    y = _dot(m_ref[...], w_ref[...])
    o_ref[...] = x_ref[...] + g_ref[...] * _rms(y, gain_ref[...])


def _out_proj(merged, w_out, x, gain, g1):
    s, d = x.shape
    tm = min(s, 512)
    vec = pl.BlockSpec((1, d), lambda i: (0, 0))
    row = pl.BlockSpec((tm, d), lambda i: (i, 0))
    return pl.pallas_call(
        _out_proj_kernel,
        grid=(s // tm,),
        in_specs=[row, pl.BlockSpec((d, d), lambda i: (0, 0)), row, vec, vec],
        out_specs=row,
        out_shape=jax.ShapeDtypeStruct((s, d), F32),
        compiler_params=_params("parallel"),
        name="out_proj",
    )(merged, w_out, x, gain, g1)


def _mlp_kernel(x_ref, gin_ref, sc_ref, sh_ref, wu_ref, wd_ref, gout_ref, g_ref, o_ref, h_ref, acc_ref):
    f = pl.program_id(1)

    @pl.when(f == 0)
    def _():
        y = _rms(x_ref[...], gin_ref[...])
        h_ref[...] = (y * (1.0 + sc_ref[...]) + sh_ref[...]).astype(h_ref.dtype)
        acc_ref[...] = jnp.zeros(acc_ref.shape, F32)

    u = jnp.square(jnp.maximum(_dot(h_ref[...], wu_ref[...]), 0.0))
    acc_ref[...] += _dot(u.astype(MXU_DTYPE), wd_ref[...])

    @pl.when(f == pl.num_programs(1) - 1)
    def _():
        o_ref[...] = x_ref[...] + g_ref[...] * _rms(acc_ref[...], gout_ref[...])


def _mlp(x, gain_in, sc, sh, w_up, w_down, gain_out, g2):
    s, d = x.shape
    tm = min(s, 512)
    tf = 1024
    vec = pl.BlockSpec((1, d), lambda i, f: (0, 0))
    row = pl.BlockSpec((tm, d), lambda i, f: (i, 0))
    return pl.pallas_call(
        _mlp_kernel,
        grid=(s // tm, D_FF // tf),
        in_specs=[row, vec, vec, vec, pl.BlockSpec((d, tf), lambda i, f: (0, f)), pl.BlockSpec((tf, d), lambda i, f: (f, 0)), vec, vec],
        out_specs=row,
        out_shape=jax.ShapeDtypeStruct((s, d), F32),
        scratch_shapes=[pltpu.VMEM((tm, d), MXU_DTYPE), pltpu.VMEM((tm, d), F32)],
        compiler_params=_params("parallel", "arbitrary"),
        name="mlp",
    )(x, gain_in, sc, sh, w_up, w_down, gain_out, g2)


def _gate_weight(w_in_l):
    w = w_in_l[:, COL_GATE:COL_GA].reshape(D_MODEL, N_KV_GROUPS, GATES_PER_GROUP)
    w = jnp.pad(w, ((0, 0), (0, 0), (0, HEAD_DIM - GATES_PER_GROUP)))
    return w.reshape(D_MODEL, N_KV_GROUPS * HEAD_DIM).astype(MXU_DTYPE)


def kernel(x, c, positions, ada_w, ada_b, norm_gains, w_in, conv_w, w_conv_out, cmp_pe, cmp_w1, cmp_b1, cmp_w2, cmp_b2,
           w_nsa_out, w_out, w_mlp_up, w_mlp_down):
    b, s, d = x.shape
    assert b == 1 and d == D_MODEL
    depth = ada_w.shape[0]
    cast = lambda w: w.astype(MXU_DTYPE)
    xs = x.reshape(s, d)
    mod = _adaln(c, ada_w, ada_b)
    tables = _rope_tables(positions)
    ks_col = COL_KV + 2 * KV_WIDTH
    for l in range(depth):
        sh1, sc1, g1, sh2, sc2, g2 = [mod[l, k * d:(k + 1) * d].reshape(1, d) for k in range(6)]
        gains = [norm_gains[l, k].reshape(1, d) for k in range(4)]
        wl = w_in[l]
        h = _norm_mod(xs, gains[0], sc1, sh1)
        cf = _conv_proj(h, cast(wl[:, :COL_Q]), conv_w[l])
        w_qkv = cast(jnp.concatenate([wl[:, COL_Q:COL_KV], wl[:, ks_col:COL_GATE]], axis=1))
        qkv = _head_proj(h, w_qkv, tables, (0, 1, 2, 3, 4, 6), (0, 1, 2, 3), MXU_DTYPE)
        kv_raw = _head_proj(h, cast(wl[:, COL_KV:ks_col]), tables, (0,), (), F32)
        gates = _gate_proj(h, _gate_weight(wl))
        kvc = _compress(kv_raw, cmp_pe[l], cast(cmp_w1[l]), cmp_b1[l], cast(cmp_w2[l]), cmp_b2[l])
        ocw, imp = _cmp_win(qkv, kvc, gates, s)
        attn = _sel_attention(qkv, _topk_bias(imp), ocw, gates, s)
        merged = _merge(h, cf, attn, cast(wl[:, COL_GA:COL_GB]), cast(wl[:, COL_GB:]), cast(w_conv_out[l]), cast(w_nsa_out[l]))
        xs = _out_proj(merged, cast(w_out[l]), xs, gains[1], g1)
        xs = _mlp(xs, gains[2], sc2, sh2, cast(w_mlp_up[l]), cast(w_mlp_down[l]), gains[3], g2)
    return xs.reshape(b, s, d)
```

```python
import functools

import jax
import jax.numpy as jnp
from jax import lax
from jax.experimental import pallas as pl
from jax.experimental.pallas import tpu as pltpu

F32 = jnp.float32
MXU_DTYPE = jnp.bfloat16

D_MODEL = 2048
CONV_WIDTH = D_MODEL // 2
N_HEADS = 16
N_KV_GROUPS = 4
HEADS_PER_GROUP = N_HEADS // N_KV_GROUPS
HEAD_DIM = D_MODEL // N_HEADS
KV_WIDTH = N_KV_GROUPS * HEAD_DIM
ROPE_DIM = HEAD_DIM // 4
ROPE_THETA = 500000.0
CMP_BLOCK = 32
CMP_STRIDE = 16
CMP_HIDDEN = 2 * HEAD_DIM
SEL_BLOCK = 64
SEL_SHIFT = 6
SEL_TOP = 16
WINDOW = 512
D_FF = 4 * D_MODEL
NORM_EPS = 1e-6
NEG = -1e30
REMOVED = -3e38
EXP2_SCALE = HEAD_DIM ** -0.5 * 1.4426950408889634

COL_Q = 3 * CONV_WIDTH
COL_KV = COL_Q + N_HEADS * HEAD_DIM
COL_GATE = COL_KV + 6 * KV_WIDTH
COL_GA = COL_GATE + 3 * N_HEADS
COL_GB = COL_GA + D_MODEL
GATES_PER_GROUP = 3 * HEADS_PER_GROUP

VMEM_LIMIT_BYTES = 56 * 1024 * 1024


def _params(*semantics):
    return pltpu.CompilerParams(dimension_semantics=semantics, vmem_limit_bytes=VMEM_LIMIT_BYTES)


def _dot(a, b):
    return jnp.dot(a, b, preferred_element_type=F32)


def _dot_nt(a, b):
    return lax.dot_general(a, b, (((1,), (1,)), ((), ())), preferred_element_type=F32)


def _rms(y, gain):
    ms = jnp.mean(y * y, axis=-1, keepdims=True)
    return y * lax.rsqrt(ms + NORM_EPS) * gain


def _adaln_kernel(c_ref, w_ref, b_ref, o_ref):
    c = c_ref[...]
    d, tn = w_ref.shape[1:]
    act = (c * jax.nn.sigmoid(c)).reshape(d // 8, 8, HEAD_DIM)
    for t in range(tn // HEAD_DIM):
        cols = slice(t * HEAD_DIM, (t + 1) * HEAD_DIM)
        part = jnp.sum(w_ref[0, :, cols].reshape(d // 8, 8, HEAD_DIM) * act, axis=0)
        o_ref[0, :, cols] = jnp.sum(part, axis=0, keepdims=True) + b_ref[0, :, cols]


def _adaln(c, ada_w, ada_b):
    depth, d, n = ada_w.shape
    tn = 1024
    c_rows = jnp.broadcast_to(c.reshape(d, 1), (d, HEAD_DIM))
    vec = pl.BlockSpec((1, 1, tn), lambda l, j: (l, 0, j))
    out = pl.pallas_call(
        _adaln_kernel,
        grid=(depth, n // tn),
        in_specs=[pl.BlockSpec((d, HEAD_DIM), lambda l, j: (0, 0)), pl.BlockSpec((1, d, tn), lambda l, j: (l, 0, j)), vec],
        out_specs=vec,
        out_shape=jax.ShapeDtypeStruct((depth, 1, n), F32),
        compiler_params=_params("parallel", "parallel"),
        name="adaln",
    )(c_rows, ada_w, ada_b.reshape(depth, 1, n))
    return out[:, 0, :]


def _rope_table_kernel(pos_ref, freq_ref, c_ref, s1_ref, s2_ref):
    half = ROPE_DIM // 2
    ang = pos_ref[...].astype(F32) * freq_ref[...]
    lane = lax.broadcasted_iota(jnp.int32, ang.shape, 1)
    cos = jnp.cos(ang)
    sin = jnp.sin(ang)
    c_ref[...] = jnp.where(lane < ROPE_DIM, cos, 1.0)
    s1_ref[...] = jnp.where(lane < half, -sin, 0.0)
    s2_ref[...] = jnp.where((lane >= half) & (lane < ROPE_DIM), sin, 0.0)


def _rope_tables(positions):
    s = positions.shape[-1]
    tm = min(s, 2048)
    half = ROPE_DIM // 2
    inv_freq = ROPE_THETA ** (-jnp.arange(0, ROPE_DIM, 2, dtype=F32) / ROPE_DIM)
    freq = jnp.concatenate([inv_freq, inv_freq, jnp.zeros((HEAD_DIM - 2 * half,), F32)]).reshape(1, HEAD_DIM)
    spec = pl.BlockSpec((tm, HEAD_DIM), lambda i: (i, 0))
    shape = jax.ShapeDtypeStruct((s, HEAD_DIM), F32)
    return pl.pallas_call(
        _rope_table_kernel,
        grid=(s // tm,),
        in_specs=[pl.BlockSpec((tm, 1), lambda i: (i, 0)), pl.BlockSpec((1, HEAD_DIM), lambda i: (0, 0))],
        out_specs=[spec, spec, spec],
        out_shape=[shape, shape, shape],
        compiler_params=_params("parallel"),
        name="rope_tables",
    )(positions.reshape(s, 1), freq)


def _rope(x, c, s1, s2):
    half = ROPE_DIM // 2
    return x * c + pltpu.roll(x, HEAD_DIM - half, axis=1) * s1 + pltpu.roll(x, half, axis=1) * s2


def _norm_mod_kernel(x_ref, g_ref, sc_ref, sh_ref, o_ref):
    y = _rms(x_ref[...], g_ref[...])
    o_ref[...] = (y * (1.0 + sc_ref[...]) + sh_ref[...]).astype(o_ref.dtype)


def _norm_mod(x, gain, sc, sh):
    s, d = x.shape
    tm = min(s, 512)
    vec = pl.BlockSpec((1, d), lambda i: (0, 0))
    return pl.pallas_call(
        _norm_mod_kernel,
        grid=(s // tm,),
        in_specs=[pl.BlockSpec((tm, d), lambda i: (i, 0)), vec, vec, vec],
        out_specs=pl.BlockSpec((tm, d), lambda i: (i, 0)),
        out_shape=jax.ShapeDtypeStruct((s, d), MXU_DTYPE),
        compiler_params=_params("parallel"),
        name="norm_mod",
    )(x, gain, sc, sh)


def _conv_proj_kernel(h_ref, wb_ref, wc_ref, wx_ref, cw_ref, o_ref, carry_ref):
    i = pl.program_id(0)
    j = pl.program_id(1)
    tm = h_ref.shape[0]
    h = h_ref[...]
    u = _dot(h, wc_ref[...]) * _dot(h, wx_ref[...])

    @pl.when(i == 0)
    def _():
        carry_ref[j] = jnp.zeros(carry_ref.shape[1:], F32)

    prev = carry_ref[j]
    carry_ref[j] = u[tm - 8:, :]
    row = lax.broadcasted_iota(jnp.int32, u.shape, 0)
    u1 = jnp.where(row == 0, prev[7:8, :], pltpu.roll(u, 1, axis=0))
    u2 = jnp.where(row == 0, prev[6:7, :], jnp.where(row == 1, prev[7:8, :], pltpu.roll(u, 2, axis=0)))
    cw = cw_ref[...]
    z = cw[2:3, :] * u + cw[1:2, :] * u1 + cw[0:1, :] * u2
    o_ref[...] = (_dot(h, wb_ref[...]) * z).astype(o_ref.dtype)


def _conv_proj(h, w_conv, conv_w):
    s, d = h.shape
    tm = min(s, 1024)
    tn = 512
    nb = CONV_WIDTH // tn
    return pl.pallas_call(
        _conv_proj_kernel,
        grid=(s // tm, nb),
        in_specs=[
            pl.BlockSpec((tm, d), lambda i, j: (i, 0)),
            pl.BlockSpec((d, tn), lambda i, j: (0, j)),
            pl.BlockSpec((d, tn), lambda i, j: (0, j + nb)),
            pl.BlockSpec((d, tn), lambda i, j: (0, j + 2 * nb)),
            pl.BlockSpec((3, tn), lambda i, j: (0, j)),
        ],
        out_specs=pl.BlockSpec((tm, tn), lambda i, j: (i, j)),
        out_shape=jax.ShapeDtypeStruct((s, CONV_WIDTH), MXU_DTYPE),
        scratch_shapes=[pltpu.VMEM((nb, 8, tn), F32)],
        compiler_params=_params("arbitrary", "arbitrary"),
        name="conv_proj",
    )(h, w_conv, w_conv, w_conv, conv_w)


def _head_proj_kernel(h_ref, w_ref, c_ref, s1_ref, s2_ref, o_ref, *, rope_groups, query_groups):
    j = pl.program_id(1)
    groups_per_step = o_ref.shape[0] // HEADS_PER_GROUP
    width = HEADS_PER_GROUP * HEAD_DIM
    any_of = lambda group, members: functools.reduce(jnp.logical_or, [group == r for r in members], False)
    for k in range(groups_per_step):
        acc = _dot(h_ref[...], w_ref[:, k * width:(k + 1) * width])
        is_rope = any_of(j * groups_per_step + k, rope_groups)
        q_scale = jnp.where(any_of(j * groups_per_step + k, query_groups), EXP2_SCALE, 1.0)
        c = jnp.where(is_rope, c_ref[...], 1.0) * q_scale
        s1 = jnp.where(is_rope, s1_ref[...], 0.0) * q_scale
        s2 = jnp.where(is_rope, s2_ref[...], 0.0) * q_scale
        for r in range(HEADS_PER_GROUP):
            head = _rope(acc[:, r * HEAD_DIM:(r + 1) * HEAD_DIM], c, s1, s2)
            o_ref[k * HEADS_PER_GROUP + r] = head.astype(o_ref.dtype)


def _head_proj(h, w, tables, rope_groups, query_groups, out_dtype):
    s, d = h.shape
    tm = min(s, 512)
    tn = min(w.shape[1], 4 * HEADS_PER_GROUP * HEAD_DIM)
    heads_per_step = tn // HEAD_DIM
    tab = pl.BlockSpec((tm, HEAD_DIM), lambda i, j: (i, 0))
    return pl.pallas_call(
        functools.partial(_head_proj_kernel, rope_groups=rope_groups, query_groups=query_groups),
        grid=(s // tm, w.shape[1] // tn),
        in_specs=[pl.BlockSpec((tm, d), lambda i, j: (i, 0)), pl.BlockSpec((d, tn), lambda i, j: (0, j)), tab, tab, tab],
        out_specs=pl.BlockSpec((heads_per_step, tm, HEAD_DIM), lambda i, j: (j, i, 0)),
        out_shape=jax.ShapeDtypeStruct((w.shape[1] // HEAD_DIM, s, HEAD_DIM), out_dtype),
        compiler_params=_params("parallel", "parallel"),
        name="head_proj",
    )(h, w, *tables)


def _gate_proj_kernel(h_ref, w_ref, o_ref):
    o_ref[...] = jax.nn.sigmoid(_dot(h_ref[...], w_ref[...]))


def _gate_proj(h, w_gate):
    s, d = h.shape
    tm = min(s, 1024)
    n = w_gate.shape[1]
    return pl.pallas_call(
        _gate_proj_kernel,
        grid=(s // tm,),
        in_specs=[pl.BlockSpec((tm, d), lambda i: (i, 0)), pl.BlockSpec((d, n), lambda i: (0, 0))],
        out_specs=pl.BlockSpec((tm, n), lambda i: (i, 0)),
        out_shape=jax.ShapeDtypeStruct((s, n), F32),
        compiler_params=_params("parallel"),
        name="gate_proj",
    )(h, w_gate)


def _compress_kernel(x_ref, pe_ref, w1_ref, b1_ref, w2_ref, b2_ref, o_ref):
    n = o_ref.shape[1]
    k_lo = CMP_STRIDE * HEAD_DIM
    pe = pe_ref[0]
    lo, hi = [], []
    for l in range(CMP_STRIDE):
        rows = x_ref[0, pl.ds(l, n, stride=CMP_STRIDE), :]
        lo.append((rows + pe[l:l + 1, :]).astype(MXU_DTYPE))
        hi.append((rows + pe[CMP_STRIDE + l:CMP_STRIDE + l + 1, :]).astype(MXU_DTYPE))
    t_lo = _dot(jnp.concatenate(lo, axis=1), w1_ref[0, :k_lo, :])
    t_hi = _dot(jnp.concatenate(hi, axis=1), w1_ref[0, k_lo:, :])
    hid = t_lo + pltpu.roll(t_hi, n - 1, axis=0) + b1_ref[0]
    out = _dot(jax.nn.gelu(hid).astype(MXU_DTYPE), w2_ref[0]) + b2_ref[0]
    row = lax.broadcasted_iota(jnp.int32, out.shape, 0)
    o_ref[0] = jnp.where(row == n - 1, 0.0, out).astype(o_ref.dtype)


def _compress(kv_raw, pe, w1, b1, w2, b2):
    nh, s, dh = kv_raw.shape
    n = s // CMP_STRIDE
    g = N_KV_GROUPS
    return pl.pallas_call(
        _compress_kernel,
        grid=(nh,),
        in_specs=[
            pl.BlockSpec((1, s, dh), lambda a: (a, 0, 0)),
            pl.BlockSpec((1, CMP_BLOCK, dh), lambda a: (a // g, 0, 0)),
            pl.BlockSpec((1, CMP_BLOCK * dh, CMP_HIDDEN), lambda a: (a // g, 0, 0)),
            pl.BlockSpec((1, 1, CMP_HIDDEN), lambda a: (a // g, 0, 0)),
            pl.BlockSpec((1, CMP_HIDDEN, dh), lambda a: (a // g, 0, 0)),
            pl.BlockSpec((1, 1, dh), lambda a: (a // g, 0, 0)),
        ],
        out_specs=pl.BlockSpec((1, n, dh), lambda a: (a, 0, 0)),
        out_shape=jax.ShapeDtypeStruct((nh, n, dh), MXU_DTYPE),
        compiler_params=_params("parallel"),
        name="compress",
    )(kv_raw, pe, w1, b1.reshape(2, 1, CMP_HIDDEN), w2, b2.reshape(2, 1, dh))


def _softmax_numerators(s, bias):
    sm = s + bias
    m = jnp.max(sm, axis=-1, keepdims=True)
    m = jnp.where(m < 0.5 * NEG, 0.0, m)
    return jnp.exp2(sm - m).astype(MXU_DTYPE)


def _cmp_win_kernel(q_ref, kc_ref, vc_ref, kw2_ref, kw1_ref, kw0_ref, vw2_ref, vw1_ref, vw0_ref, ov_ref, gate_ref,
                    ocw_ref, imp_ref, *, tq, chunk):
    n_cmp = kc_ref.shape[1]
    last_visible = ((pl.program_id(1) + 1) * tq - CMP_BLOCK) // CMP_STRIDE
    for c in range(n_cmp // chunk):
        pl.when(jnp.maximum(last_visible, 0) // chunk == c)(functools.partial(
            _cmp_win_body, q_ref, kc_ref, vc_ref, kw2_ref, kw1_ref, kw0_ref, vw2_ref, vw1_ref, vw0_ref, ov_ref, gate_ref,
            ocw_ref, imp_ref, tq=tq, n_cmp=(c + 1) * chunk))


def _cmp_win_body(q_ref, kc_ref, vc_ref, kw2_ref, kw1_ref, kw0_ref, vw2_ref, vw1_ref, vw0_ref, ov_ref, gate_ref,
                  ocw_ref, imp_ref, *, tq, n_cmp):
    i = pl.program_id(1)
    r_heads = HEADS_PER_GROUP
    s0 = i * tq
    q4 = q_ref[...].reshape(r_heads * tq, HEAD_DIM)
    tok = s0 + lax.broadcasted_iota(jnp.int32, (tq, 1), 0)

    kwin = jnp.concatenate([kw2_ref[0], kw1_ref[0], kw0_ref[0]], axis=0)
    s_w = _dot_nt(q4, kwin)
    s_c = _dot_nt(q4, kc_ref[0, :n_cmp, :])
    vwin = jnp.concatenate([vw2_ref[0], vw1_ref[0], vw0_ref[0]], axis=0)
    wpos = s0 - 2 * tq + lax.broadcasted_iota(jnp.int32, (1, 3 * tq), 1)
    diff = tok - wpos
    bias_w = jnp.where((diff >= 0) & (diff < WINDOW) & (wpos >= 0), 0.0, NEG)
    p_w = jnp.concatenate([_softmax_numerators(s_w[r * tq:(r + 1) * tq, :], bias_w) for r in range(r_heads)], axis=0)
    res_w = _dot(p_w, jnp.concatenate([vwin, jnp.ones((3 * tq, HEAD_DIM), MXU_DTYPE)], axis=1))
    cmp_end = lax.broadcasted_iota(jnp.int32, (1, n_cmp), 1) * CMP_STRIDE + (CMP_BLOCK - 1)
    bias_c = jnp.where(cmp_end <= tok, 0.0, NEG)
    p_c = jnp.concatenate([_softmax_numerators(s_c[r * tq:(r + 1) * tq, :], bias_c) for r in range(r_heads)], axis=0)
    ones_c = jnp.ones((n_cmp, HEAD_DIM), MXU_DTYPE)
    res_c = _dot(p_c, jnp.concatenate([vc_ref[0, :n_cmp, :], ones_c, ov_ref[:n_cmp, :]], axis=1))
    o_c, imp = [], None
    for r in range(r_heads):
        rows = slice(r * tq, (r + 1) * tq)
        l = res_c[rows, HEAD_DIM:2 * HEAD_DIM]
        inv = jnp.where(l > 0.0, 1.0 / l, 0.0)
        o_c.append(res_c[rows, :HEAD_DIM] * inv)
        part = res_c[rows, 2 * HEAD_DIM:] * inv[:, :1]
        imp = part if imp is None else imp + part
    imp_ref[0] = imp

    gate = gate_ref[...]
    for r in range(r_heads):
        rows = slice(r * tq, (r + 1) * tq)
        g_cmp = gate[:, 3 * r:3 * r + 1]
        g_win = gate[:, 3 * r + 2:3 * r + 3]
        o_w = res_w[rows, :HEAD_DIM] / res_w[rows, HEAD_DIM:]
        ocw_ref[:, r * HEAD_DIM:(r + 1) * HEAD_DIM] = g_cmp * o_c[r] + g_win * o_w


def _overlap_matrix(n_cmp_rows, n_slc):
    cs = jnp.arange(n_cmp_rows)[:, None] * CMP_STRIDE
    ss = jnp.arange(n_slc)[None, :] * SEL_BLOCK
    ov = jnp.clip(jnp.minimum(cs + CMP_BLOCK, ss + SEL_BLOCK) - jnp.maximum(cs, ss), 0, None)
    return (ov.astype(F32) / CMP_BLOCK).astype(MXU_DTYPE)


def _cmp_win(qkv, kvc, gates, s):
    tq = min(s, 256)
    assert WINDOW <= 2 * tq
    g = N_KV_GROUPS
    n_cmp = kvc.shape[1]
    n_slc = s // SEL_BLOCK
    head = lambda base: [
        pl.BlockSpec((1, tq, HEAD_DIM), functools.partial(lambda a, i, d, base: (base + a, jnp.maximum(i - d, 0), 0), d=d, base=base))
        for d in (2, 1, 0)
    ]
    ocw, imp = pl.pallas_call(
        functools.partial(_cmp_win_kernel, tq=tq, chunk=min(n_cmp, 256)),
        grid=(g, s // tq),
        in_specs=[
            pl.BlockSpec((HEADS_PER_GROUP, tq, HEAD_DIM), lambda a, i: (a, i, 0)),
            pl.BlockSpec((1, n_cmp, HEAD_DIM), lambda a, i: (a, 0, 0)),
            pl.BlockSpec((1, n_cmp, HEAD_DIM), lambda a, i: (g + a, 0, 0)),
            *head(N_HEADS + 2 * g),
            *head(N_HEADS + 3 * g),
            pl.BlockSpec((n_cmp, n_slc), lambda a, i: (0, 0)),
            pl.BlockSpec((tq, HEAD_DIM), lambda a, i: (i, a)),
        ],
        out_specs=[
            pl.BlockSpec((tq, HEADS_PER_GROUP * HEAD_DIM), lambda a, i: (i, a)),
            pl.BlockSpec((1, tq, n_slc), lambda a, i: (a, i, 0)),
        ],
        out_shape=[
            jax.ShapeDtypeStruct((s, N_HEADS * HEAD_DIM), F32),
            jax.ShapeDtypeStruct((g, s, n_slc), F32),
        ],
        compiler_params=_params("parallel", "parallel"),
        name="cmp_win",
    )(qkv, kvc, kvc, qkv, qkv, qkv, qkv, qkv, qkv, _overlap_matrix(n_cmp, n_slc), gates)
    return ocw, imp


def _topk_kernel(imp_ref, sb_ref, work_ref):
    tm, n_slc = work_ref.shape
    tok = pl.program_id(1) * tm + lax.broadcasted_iota(jnp.int32, (tm, 1), 0)
    blk = lax.broadcasted_iota(jnp.int32, (1, n_slc), 1)
    cur = tok >> SEL_SHIFT
    forced = (blk == cur) | (blk == 0)
    candidate = (blk <= cur) & jnp.logical_not(forced)
    n_pick = min(SEL_TOP, n_slc) - 2
    start = lambda: jnp.where(forced, REMOVED, jnp.where(blk <= cur, imp_ref[0], NEG))

    work_ref[...] = start()
    for _ in range(n_pick):
        w = work_ref[...]
        work_ref[...] = jnp.where(w == jnp.max(w, axis=-1, keepdims=True), REMOVED, w)
    taken = jnp.sum(jnp.where((work_ref[...] == REMOVED) & candidate, 1.0, 0.0))
    expected = jnp.sum(jnp.minimum(jnp.maximum(cur - 1, 0), n_pick).astype(F32))

    @pl.when(taken > expected)
    def _():
        blk_f = blk.astype(F32)
        work_ref[...] = start()
        for _ in range(n_pick):
            w = work_ref[...]
            best = jnp.max(w, axis=-1, keepdims=True)
            idx = jnp.min(jnp.where(w == best, blk_f, float(n_slc)), axis=-1, keepdims=True)
            work_ref[...] = jnp.where(blk_f == idx, REMOVED, w)

    sb_ref[0] = jnp.where(work_ref[...] == REMOVED, 0.0, NEG).astype(sb_ref.dtype)


def _topk_bias(imp):
    g, s, n_slc = imp.shape
    tm = min(s, 1024)
    spec = pl.BlockSpec((1, tm, n_slc), lambda a, i: (a, i, 0))
    return pl.pallas_call(
        _topk_kernel,
        grid=(g, s // tm),
        in_specs=[spec],
        out_specs=spec,
        out_shape=jax.ShapeDtypeStruct((g, s, n_slc), MXU_DTYPE),
        scratch_shapes=[pltpu.VMEM((tm, n_slc), F32)],
        compiler_params=_params("parallel", "parallel"),
        name="topk_bias",
    )(imp)


def _sel_kernel(q_ref, k_ref, v_ref, sb_ref, ocw_ref, gate_ref, o_ref, lhs_ref, s_ref, m_ref, acc_ref, *, tq, tk):
    i = pl.program_id(1)
    r_heads = HEADS_PER_GROUP
    bias_lanes = lhs_ref.shape[2] - HEAD_DIM
    blocks_per_tile = tk // SEL_BLOCK
    lane_tiles = tk // HEAD_DIM
    n_tiles = (i * tq + tq - 1) // tk + 1
    tok = i * tq + lax.broadcasted_iota(jnp.int32, (tq, 1), 0)
    key_blk = lax.broadcasted_iota(jnp.int32, (tk, bias_lanes), 0) >> SEL_SHIFT
    lane = lax.broadcasted_iota(jnp.int32, (tk, bias_lanes), 1)
    key_off = lax.broadcasted_iota(jnp.int32, (1, tk), 1)

    for c in range(lhs_ref.shape[0]):
        for r in range(r_heads):
            rows = slice(r * tq, (r + 1) * tq)
            lhs_ref[c, rows, :HEAD_DIM] = q_ref[r]
            lhs_ref[c, rows, HEAD_DIM:] = sb_ref[0, :, c * bias_lanes:(c + 1) * bias_lanes]

    def produce(slot, j, parts=1):
        start = pl.multiple_of(j * tk, tk)
        first_blk = j * blocks_per_tile
        onehot = jnp.where(lane == first_blk % bias_lanes + key_blk, 1.0, 0.0).astype(MXU_DTYPE)
        k_ext = jnp.concatenate([k_ref[0, pl.ds(start, tk), :], onehot], axis=1)
        n = r_heads * tq // parts
        for h in range(parts):
            s_ref[slot, h * n:(h + 1) * n, :] = _dot_nt(lhs_ref[first_blk // bias_lanes, h * n:(h + 1) * n, :], k_ext)

    def consume(slot, j, diagonal, parts=1):
        start = pl.multiple_of(j * tk, tk)
        v_ext = jnp.concatenate([v_ref[0, pl.ds(start, tk), :], jnp.ones((tk, HEAD_DIM), MXU_DTYPE)], axis=1)
        p_rows, alphas = [], []
        for r in range(r_heads):
            rows = slice(r * tq, (r + 1) * tq)
            sm = s_ref[slot, rows, :]
            if diagonal:
                sm = jnp.where(j * tk + key_off <= tok, sm, NEG)
            m_prev = m_ref[rows, :]
            m_new = jnp.maximum(m_prev, jnp.max(sm, axis=-1, keepdims=True))
            m_ref[rows, :] = m_new
            p_rows.append(jnp.exp2(sm - jnp.concatenate([m_new] * lane_tiles, axis=1)).astype(MXU_DTYPE))
            alphas.append(jnp.exp2(m_prev - m_new))
        per = r_heads // parts
        for h in range(parts):
            pv = _dot(jnp.concatenate(p_rows[h * per:(h + 1) * per], axis=0), v_ext)
            for r in range(per):
                rows = slice((h * per + r) * tq, (h * per + r + 1) * tq)
                scale = jnp.concatenate([alphas[h * per + r]] * 2, axis=1)
                acc_ref[rows, :] = acc_ref[rows, :] * scale + pv[r * tq:(r + 1) * tq, :]

    produce(0, 0, parts=2)
    m_ref[...] = jnp.full(m_ref.shape, NEG, F32)
    acc_ref[...] = jnp.zeros(acc_ref.shape, F32)

    def body(pair, carry):
        j = 2 * pair
        consume(0, j, diagonal=False)
        produce(1, j + 1)
        consume(1, j + 1, diagonal=False)
        produce(0, j + 2)
        return carry

    n_pairs = (n_tiles - 1) // 2
    lax.fori_loop(0, n_pairs, body, 0)
    done = 2 * n_pairs

    @pl.when(n_tiles - done == 1)
    def _():
        consume(0, done, diagonal=True, parts=2)

    @pl.when(n_tiles - done == 2)
    def _():
        consume(0, done, diagonal=False)
        produce(1, done + 1)
        consume(1, done + 1, diagonal=True, parts=2)

    gate = gate_ref[...]
    for r in range(r_heads):
        rows = slice(r * tq, (r + 1) * tq)
        cols = slice(r * HEAD_DIM, (r + 1) * HEAD_DIM)
        o_s = acc_ref[rows, :HEAD_DIM] / acc_ref[rows, HEAD_DIM:]
        o_ref[:, cols] = (ocw_ref[:, cols] + gate[:, 3 * r + 1:3 * r + 2] * o_s).astype(o_ref.dtype)


def _sel_attention(qkv, sel_bias, ocw, gates, s):
    tq = min(s, 512)
    tk = min(s, 512)
    g = N_KV_GROUPS
    n_slc = s // SEL_BLOCK
    width = HEADS_PER_GROUP * HEAD_DIM
    rows = HEADS_PER_GROUP * tq
    bias_lanes = min(HEAD_DIM, n_slc)
    assert n_slc % bias_lanes == 0 and bias_lanes % (tk // SEL_BLOCK) == 0 and tk % tq == 0
    return pl.pallas_call(
        functools.partial(_sel_kernel, tq=tq, tk=tk),
        grid=(g, s // tq),
        in_specs=[
            pl.BlockSpec((HEADS_PER_GROUP, tq, HEAD_DIM), lambda a, i: (a, i, 0)),
            pl.BlockSpec((1, s, HEAD_DIM), lambda a, i: (N_HEADS + a, 0, 0)),
            pl.BlockSpec((1, s, HEAD_DIM), lambda a, i: (N_HEADS + g + a, 0, 0)),
            pl.BlockSpec((1, tq, n_slc), lambda a, i: (a, i, 0)),
            pl.BlockSpec((tq, width), lambda a, i: (i, a)),
            pl.BlockSpec((tq, HEAD_DIM), lambda a, i: (i, a)),
        ],
        out_specs=pl.BlockSpec((tq, width), lambda a, i: (i, a)),
        out_shape=jax.ShapeDtypeStruct((s, N_HEADS * HEAD_DIM), MXU_DTYPE),
        scratch_shapes=[
            pltpu.VMEM((n_slc // bias_lanes, rows, HEAD_DIM + bias_lanes), MXU_DTYPE),
            pltpu.VMEM((2, rows, tk), F32),
            pltpu.VMEM((rows, HEAD_DIM), F32),
            pltpu.VMEM((rows, 2 * HEAD_DIM), F32),
        ],
        compiler_params=_params("parallel", "arbitrary"),
        name="sel_attention",
    )(qkv, qkv, qkv, sel_bias, ocw, gates)


def _merge_kernel(h_ref, cf_ref, at_ref, wga_ref, wgb_ref, wco_ref, wno_ref, o_ref):
    h = h_ref[...]
    y_a = _dot(cf_ref[...], wco_ref[...])
    y_b = _dot(at_ref[...], wno_ref[...])
    merged = jax.nn.sigmoid(_dot(h, wga_ref[...])) * y_a + jax.nn.sigmoid(_dot(h, wgb_ref[...])) * y_b
    o_ref[...] = merged.astype(o_ref.dtype)


def _merge(h, cf, attn, w_ga, w_gb, w_co, w_no):
    s, d = h.shape
    tm = min(s, 1024)
    tn = 512
    row = lambda width: pl.BlockSpec((tm, width), lambda i, j: (i, 0))
    col = lambda depth: pl.BlockSpec((depth, tn), lambda i, j: (0, j))
    return pl.pallas_call(
        _merge_kernel,
        grid=(s // tm, d // tn),
        in_specs=[row(d), row(CONV_WIDTH), row(N_HEADS * HEAD_DIM), col(d), col(d), col(CONV_WIDTH), col(N_HEADS * HEAD_DIM)],
        out_specs=pl.BlockSpec((tm, tn), lambda i, j: (i, j)),
        out_shape=jax.ShapeDtypeStruct((s, d), MXU_DTYPE),
        compiler_params=_params("parallel", "parallel"),
        name="merge",
    )(h, cf, attn, w_ga, w_gb, w_co, w_no)


def _out_proj_kernel(m_ref, w_ref, x_ref, gain_ref, g_ref, o_ref):
    y = _dot(m_ref[...], w_ref[...])
    o_ref[...] = x_ref[...] + g_ref[...] * _rms(y, gain_ref[...])


def _out_proj(merged, w_out, x, gain, g1):
    s, d = x.shape
    tm = min(s, 512)
    vec = pl.BlockSpec((1, d), lambda i: (0, 0))
    row = pl.BlockSpec((tm, d), lambda i: (i, 0))
    return pl.pallas_call(
        _out_proj_kernel,
        grid=(s // tm,),
        in_specs=[row, pl.BlockSpec((d, d), lambda i: (0, 0)), row, vec, vec],
        out_specs=row,
        out_shape=jax.ShapeDtypeStruct((s, d), F32),
        compiler_params=_params("parallel"),
        name="out_proj",
    )(merged, w_out, x, gain, g1)


def _mlp_kernel(x_ref, gin_ref, sc_ref, sh_ref, wu_ref, wd_ref, gout_ref, g_ref, o_ref, h_ref, acc_ref):
    f = pl.program_id(1)

    @pl.when(f == 0)
    def _():
        y = _rms(x_ref[...], gin_ref[...])
        h_ref[...] = (y * (1.0 + sc_ref[...]) + sh_ref[...]).astype(h_ref.dtype)
        acc_ref[...] = jnp.zeros(acc_ref.shape, F32)

    u = jnp.square(jnp.maximum(_dot(h_ref[...], wu_ref[...]), 0.0))
    acc_ref[...] += _dot(u.astype(MXU_DTYPE), wd_ref[...])

    @pl.when(f == pl.num_programs(1) - 1)
    def _():
        o_ref[...] = x_ref[...] + g_ref[...] * _rms(acc_ref[...], gout_ref[...])


def _mlp(x, gain_in, sc, sh, w_up, w_down, gain_out, g2):
    s, d = x.shape
    tm = min(s, 512)
    tf = 1024
    vec = pl.BlockSpec((1, d), lambda i, f: (0, 0))
    row = pl.BlockSpec((tm, d), lambda i, f: (i, 0))
    return pl.pallas_call(
        _mlp_kernel,
        grid=(s // tm, D_FF // tf),
        in_specs=[row, vec, vec, vec, pl.BlockSpec((d, tf), lambda i, f: (0, f)), pl.BlockSpec((tf, d), lambda i, f: (f, 0)), vec, vec],
        out_specs=row,
        out_shape=jax.ShapeDtypeStruct((s, d), F32),
        scratch_shapes=[pltpu.VMEM((tm, d), MXU_DTYPE), pltpu.VMEM((tm, d), F32)],
        compiler_params=_params("parallel", "arbitrary"),
        name="mlp",
    )(x, gain_in, sc, sh, w_up, w_down, gain_out, g2)


def _gate_weight(w_in_l):
    w = w_in_l[:, COL_GATE:COL_GA].reshape(D_MODEL, N_KV_GROUPS, GATES_PER_GROUP)
    w = jnp.pad(w, ((0, 0), (0, 0), (0, HEAD_DIM - GATES_PER_GROUP)))
    return w.reshape(D_MODEL, N_KV_GROUPS * HEAD_DIM).astype(MXU_DTYPE)


def kernel(x, c, positions, ada_w, ada_b, norm_gains, w_in, conv_w, w_conv_out, cmp_pe, cmp_w1, cmp_b1, cmp_w2, cmp_b2,
           w_nsa_out, w_out, w_mlp_up, w_mlp_down):
    b, s, d = x.shape
    assert b == 1 and d == D_MODEL
    depth = ada_w.shape[0]
    cast = lambda w: w.astype(MXU_DTYPE)
    xs = x.reshape(s, d)
    mod = _adaln(c, ada_w, ada_b)
    tables = _rope_tables(positions)
    ks_col = COL_KV + 2 * KV_WIDTH
    for l in range(depth):
        sh1, sc1, g1, sh2, sc2, g2 = [mod[l, k * d:(k + 1) * d].reshape(1, d) for k in range(6)]
        gains = [norm_gains[l, k].reshape(1, d) for k in range(4)]
        wl = w_in[l]
        h = _norm_mod(xs, gains[0], sc1, sh1)
        cf = _conv_proj(h, cast(wl[:, :COL_Q]), conv_w[l])
        w_qkv = cast(jnp.concatenate([wl[:, COL_Q:COL_KV], wl[:, ks_col:COL_GATE]], axis=1))
        qkv = _head_proj(h, w_qkv, tables, (0, 1, 2, 3, 4, 6), (0, 1, 2, 3), MXU_DTYPE)
        kv_raw = _head_proj(h, cast(wl[:, COL_KV:ks_col]), tables, (0,), (), F32)
        gates = _gate_proj(h, _gate_weight(wl))
        kvc = _compress(kv_raw, cmp_pe[l], cast(cmp_w1[l]), cmp_b1[l], cast(cmp_w2[l]), cmp_b2[l])
        ocw, imp = _cmp_win(qkv, kvc, gates, s)
        attn = _sel_attention(qkv, _topk_bias(imp), ocw, gates, s)
        merged = _merge(h, cf, attn, cast(wl[:, COL_GA:COL_GB]), cast(wl[:, COL_GB:]), cast(w_conv_out[l]), cast(w_nsa_out[l]))
        xs = _out_proj(merged, cast(w_out[l]), xs, gains[1], g1)
        xs = _mlp(xs, gains[2], sc2, sh2, cast(w_mlp_up[l]), cast(w_mlp_down[l]), gains[3], g2)
    return xs.reshape(b, s, d)
```

```python
import functools

import jax
import jax.numpy as jnp
from jax import lax
from jax.experimental import pallas as pl
from jax.experimental.pallas import tpu as pltpu

F32 = jnp.float32
MXU_DTYPE = jnp.bfloat16

D_MODEL = 2048
CONV_WIDTH = D_MODEL // 2
N_HEADS = 16
N_KV_GROUPS = 4
HEADS_PER_GROUP = N_HEADS // N_KV_GROUPS
HEAD_DIM = D_MODEL // N_HEADS
KV_WIDTH = N_KV_GROUPS * HEAD_DIM
ROPE_DIM = HEAD_DIM // 4
ROPE_THETA = 500000.0
CMP_BLOCK = 32
CMP_STRIDE = 16
CMP_HIDDEN = 2 * HEAD_DIM
SEL_BLOCK = 64
SEL_SHIFT = 6
SEL_TOP = 16
WINDOW = 512
D_FF = 4 * D_MODEL
NORM_EPS = 1e-6
NEG = -1e30
REMOVED = -3e38
EXP2_SCALE = HEAD_DIM ** -0.5 * 1.4426950408889634

COL_Q = 3 * CONV_WIDTH
COL_KV = COL_Q + N_HEADS * HEAD_DIM
COL_GATE = COL_KV + 6 * KV_WIDTH
COL_GA = COL_GATE + 3 * N_HEADS
COL_GB = COL_GA + D_MODEL
GATES_PER_GROUP = 3 * HEADS_PER_GROUP

VMEM_LIMIT_BYTES = 56 * 1024 * 1024
LANES = 128
SUBLANES = 8
PACKED_ROWS = 16
MXU_WIDTH = 256
assert HEAD_DIM == LANES


def _params(*semantics):
    return pltpu.CompilerParams(dimension_semantics=semantics, vmem_limit_bytes=VMEM_LIMIT_BYTES)


def _dot(a, b):
    return jnp.dot(a, b, preferred_element_type=F32)


def _dot_nt(a, b):
    return lax.dot_general(a, b, (((1,), (1,)), ((), ())), preferred_element_type=F32)


def _rms(y, gain):
    ms = jnp.mean(y * y, axis=-1, keepdims=True)
    return y * lax.rsqrt(ms + NORM_EPS) * gain


def _adaln_kernel(c_ref, w_ref, b_ref, o_ref):
    c = c_ref[...]
    d, tn = w_ref.shape[1:]
    act = (c * jax.nn.sigmoid(c)).reshape(d // SUBLANES, SUBLANES, LANES)
    for t in range(tn // LANES):
        cols = slice(t * LANES, (t + 1) * LANES)
        part = jnp.sum(w_ref[0, :, cols].reshape(d // SUBLANES, SUBLANES, LANES) * act, axis=0)
        o_ref[0, :, cols] = jnp.sum(part, axis=0, keepdims=True) + b_ref[0, :, cols]


def _adaln(c, ada_w, ada_b):
    depth, d, n = ada_w.shape
    tn = 1024
    c_rows = jnp.broadcast_to(c.reshape(d, 1), (d, LANES))
    vec = pl.BlockSpec((1, 1, tn), lambda l, j: (l, 0, j))
    out = pl.pallas_call(
        _adaln_kernel,
        grid=(depth, n // tn),
        in_specs=[pl.BlockSpec((d, LANES), lambda l, j: (0, 0)), pl.BlockSpec((1, d, tn), lambda l, j: (l, 0, j)), vec],
        out_specs=vec,
        out_shape=jax.ShapeDtypeStruct((depth, 1, n), F32),
        compiler_params=_params("parallel", "parallel"),
        name="adaln",
    )(c_rows, ada_w, ada_b.reshape(depth, 1, n))
    return out[:, 0, :]


def _rope_table_kernel(pos_ref, freq_ref, c_ref, s1_ref, s2_ref):
    half = ROPE_DIM // 2
    ang = pos_ref[...].astype(F32) * freq_ref[...]
    lane = lax.broadcasted_iota(jnp.int32, ang.shape, 1)
    cos = jnp.cos(ang)
    sin = jnp.sin(ang)
    c_ref[...] = jnp.where(lane < ROPE_DIM, cos, 1.0)
    s1_ref[...] = jnp.where(lane < half, -sin, 0.0)
    s2_ref[...] = jnp.where((lane >= half) & (lane < ROPE_DIM), sin, 0.0)


def _rope_tables(positions):
    s = positions.shape[-1]
    tm = min(s, 2048)
    half = ROPE_DIM // 2
    inv_freq = ROPE_THETA ** (-jnp.arange(0, ROPE_DIM, 2, dtype=F32) / ROPE_DIM)
    freq = jnp.concatenate([inv_freq, inv_freq, jnp.zeros((HEAD_DIM - 2 * half,), F32)]).reshape(1, HEAD_DIM)
    spec = pl.BlockSpec((tm, HEAD_DIM), lambda i: (i, 0))
    shape = jax.ShapeDtypeStruct((s, HEAD_DIM), F32)
    return pl.pallas_call(
        _rope_table_kernel,
        grid=(s // tm,),
        in_specs=[pl.BlockSpec((tm, 1), lambda i: (i, 0)), pl.BlockSpec((1, HEAD_DIM), lambda i: (0, 0))],
        out_specs=[spec, spec, spec],
        out_shape=[shape, shape, shape],
        compiler_params=_params("parallel"),
        name="rope_tables",
    )(positions.reshape(s, 1), freq)


def _rope(x, c, s1, s2):
    half = ROPE_DIM // 2
    return x * c + pltpu.roll(x, HEAD_DIM - half, axis=1) * s1 + pltpu.roll(x, half, axis=1) * s2


def _norm_mod_kernel(x_ref, g_ref, sc_ref, sh_ref, o_ref):
    y = _rms(x_ref[...], g_ref[...])
    o_ref[...] = (y * (1.0 + sc_ref[...]) + sh_ref[...]).astype(o_ref.dtype)


def _norm_mod(x, gain, sc, sh):
    s, d = x.shape
    tm = min(s, 512)
    vec = pl.BlockSpec((1, d), lambda i: (0, 0))
    return pl.pallas_call(
        _norm_mod_kernel,
        grid=(s // tm,),
        in_specs=[pl.BlockSpec((tm, d), lambda i: (i, 0)), vec, vec, vec],
        out_specs=pl.BlockSpec((tm, d), lambda i: (i, 0)),
        out_shape=jax.ShapeDtypeStruct((s, d), MXU_DTYPE),
        compiler_params=_params("parallel"),
        name="norm_mod",
    )(x, gain, sc, sh)


def _conv_proj_kernel(h_ref, wb_ref, wc_ref, wx_ref, cw_ref, o_ref, carry_ref):
    i = pl.program_id(0)
    j = pl.program_id(1)
    tm = h_ref.shape[0]
    h = h_ref[...]
    u = _dot(h, wc_ref[...]) * _dot(h, wx_ref[...])

    @pl.when(i == 0)
    def _():
        carry_ref[j] = jnp.zeros(carry_ref.shape[1:], F32)

    prev = carry_ref[j]
    carry_ref[j] = u[tm - SUBLANES:, :]
    row = lax.broadcasted_iota(jnp.int32, u.shape, 0)
    last, before_last = prev[SUBLANES - 1:, :], prev[SUBLANES - 2:SUBLANES - 1, :]
    u1 = jnp.where(row == 0, last, pltpu.roll(u, 1, axis=0))
    u2 = jnp.where(row == 0, before_last, jnp.where(row == 1, last, pltpu.roll(u, 2, axis=0)))
    cw = cw_ref[...]
    z = cw[2:3, :] * u + cw[1:2, :] * u1 + cw[0:1, :] * u2
    o_ref[...] = (_dot(h, wb_ref[...]) * z).astype(o_ref.dtype)


def _conv_proj(h, w_conv, conv_w):
    s, d = h.shape
    tm = min(s, 1024)
    tn = 512
    nb = CONV_WIDTH // tn
    return pl.pallas_call(
        _conv_proj_kernel,
        grid=(s // tm, nb),
        in_specs=[
            pl.BlockSpec((tm, d), lambda i, j: (i, 0)),
            pl.BlockSpec((d, tn), lambda i, j: (0, j)),
            pl.BlockSpec((d, tn), lambda i, j: (0, j + nb)),
            pl.BlockSpec((d, tn), lambda i, j: (0, j + 2 * nb)),
            pl.BlockSpec((3, tn), lambda i, j: (0, j)),
        ],
        out_specs=pl.BlockSpec((tm, tn), lambda i, j: (i, j)),
        out_shape=jax.ShapeDtypeStruct((s, CONV_WIDTH), MXU_DTYPE),
        scratch_shapes=[pltpu.VMEM((nb, SUBLANES, tn), F32)],
        compiler_params=_params("arbitrary", "arbitrary"),
        name="conv_proj",
    )(h, w_conv, w_conv, w_conv, conv_w)


def _head_proj_kernel(h_ref, w_ref, c_ref, s1_ref, s2_ref, o_ref, *, rope_groups, query_groups):
    j = pl.program_id(1)
    groups_per_step = o_ref.shape[0] // HEADS_PER_GROUP
    width = HEADS_PER_GROUP * HEAD_DIM
    any_of = lambda group, members: functools.reduce(jnp.logical_or, [group == r for r in members], False)
    for k in range(groups_per_step):
        acc = _dot(h_ref[...], w_ref[:, k * width:(k + 1) * width])
        is_rope = any_of(j * groups_per_step + k, rope_groups)
        q_scale = jnp.where(any_of(j * groups_per_step + k, query_groups), EXP2_SCALE, 1.0)
        c = jnp.where(is_rope, c_ref[...], 1.0) * q_scale
        s1 = jnp.where(is_rope, s1_ref[...], 0.0) * q_scale
        s2 = jnp.where(is_rope, s2_ref[...], 0.0) * q_scale
        for r in range(HEADS_PER_GROUP):
            head = _rope(acc[:, r * HEAD_DIM:(r + 1) * HEAD_DIM], c, s1, s2)
            o_ref[k * HEADS_PER_GROUP + r] = head.astype(o_ref.dtype)


def _head_proj(h, w, tables, rope_groups, query_groups, out_dtype):
    s, d = h.shape
    tm = min(s, 512)
    tn = min(w.shape[1], 4 * HEADS_PER_GROUP * HEAD_DIM)
    heads_per_step = tn // HEAD_DIM
    tab = pl.BlockSpec((tm, HEAD_DIM), lambda i, j: (i, 0))
    return pl.pallas_call(
        functools.partial(_head_proj_kernel, rope_groups=rope_groups, query_groups=query_groups),
        grid=(s // tm, w.shape[1] // tn),
        in_specs=[pl.BlockSpec((tm, d), lambda i, j: (i, 0)), pl.BlockSpec((d, tn), lambda i, j: (0, j)), tab, tab, tab],
        out_specs=pl.BlockSpec((heads_per_step, tm, HEAD_DIM), lambda i, j: (j, i, 0)),
        out_shape=jax.ShapeDtypeStruct((w.shape[1] // HEAD_DIM, s, HEAD_DIM), out_dtype),
        compiler_params=_params("parallel", "parallel"),
        name="head_proj",
    )(h, w, *tables)


def _gate_proj_kernel(h_ref, w_ref, o_ref):
    o_ref[...] = jax.nn.sigmoid(_dot(h_ref[...], w_ref[...]))


def _gate_proj(h, w_gate):
    s, d = h.shape
    tm = min(s, 1024)
    n = w_gate.shape[1]
    return pl.pallas_call(
        _gate_proj_kernel,
        grid=(s // tm,),
        in_specs=[pl.BlockSpec((tm, d), lambda i: (i, 0)), pl.BlockSpec((d, n), lambda i: (0, 0))],
        out_specs=pl.BlockSpec((tm, n), lambda i: (i, 0)),
        out_shape=jax.ShapeDtypeStruct((s, n), F32),
        compiler_params=_params("parallel"),
        name="gate_proj",
    )(h, w_gate)


def _compress_kernel(x_ref, pe_ref, w1_ref, b1_ref, w2_ref, b2_ref, o_ref):
    n = o_ref.shape[1]
    k_lo = CMP_STRIDE * HEAD_DIM
    pe = pe_ref[0]
    lo, hi = [], []
    for l in range(CMP_STRIDE):
        rows = x_ref[0, pl.ds(l, n, stride=CMP_STRIDE), :]
        lo.append((rows + pe[l:l + 1, :]).astype(MXU_DTYPE))
        hi.append((rows + pe[CMP_STRIDE + l:CMP_STRIDE + l + 1, :]).astype(MXU_DTYPE))
    t_lo = _dot(jnp.concatenate(lo, axis=1), w1_ref[0, :k_lo, :])
    t_hi = _dot(jnp.concatenate(hi, axis=1), w1_ref[0, k_lo:, :])
    hid = t_lo + pltpu.roll(t_hi, n - 1, axis=0) + b1_ref[0]
    out = _dot(jax.nn.gelu(hid).astype(MXU_DTYPE), w2_ref[0]) + b2_ref[0]
    row = lax.broadcasted_iota(jnp.int32, out.shape, 0)
    o_ref[0] = jnp.where(row == n - 1, 0.0, out).astype(o_ref.dtype)


def _compress(kv_raw, pe, w1, b1, w2, b2):
    nh, s, dh = kv_raw.shape
    n = s // CMP_STRIDE
    g = N_KV_GROUPS
    return pl.pallas_call(
        _compress_kernel,
        grid=(nh,),
        in_specs=[
            pl.BlockSpec((1, s, dh), lambda a: (a, 0, 0)),
            pl.BlockSpec((1, CMP_BLOCK, dh), lambda a: (a // g, 0, 0)),
            pl.BlockSpec((1, CMP_BLOCK * dh, CMP_HIDDEN), lambda a: (a // g, 0, 0)),
            pl.BlockSpec((1, 1, CMP_HIDDEN), lambda a: (a // g, 0, 0)),
            pl.BlockSpec((1, CMP_HIDDEN, dh), lambda a: (a // g, 0, 0)),
            pl.BlockSpec((1, 1, dh), lambda a: (a // g, 0, 0)),
        ],
        out_specs=pl.BlockSpec((1, n, dh), lambda a: (a, 0, 0)),
        out_shape=jax.ShapeDtypeStruct((nh, n, dh), MXU_DTYPE),
        compiler_params=_params("parallel"),
        name="compress",
    )(kv_raw, pe, w1, b1.reshape(2, 1, CMP_HIDDEN), w2, b2.reshape(2, 1, dh))


def _softmax_numerators(s, bias):
    sm = s + bias
    m = jnp.max(sm, axis=-1, keepdims=True)
    m = jnp.where(m < 0.5 * NEG, 0.0, m)
    return jnp.exp2(sm - m).astype(MXU_DTYPE)


def _cmp_win_kernel(q_ref, kc_ref, vc_ref, kw2_ref, kw1_ref, kw0_ref, vw2_ref, vw1_ref, vw0_ref, ov_ref, gate_ref,
                    ocw_ref, imp_ref, *, tq, chunk):
    n_cmp = kc_ref.shape[1]
    last_visible = ((pl.program_id(1) + 1) * tq - CMP_BLOCK) // CMP_STRIDE
    for c in range(n_cmp // chunk):
        pl.when(jnp.maximum(last_visible, 0) // chunk == c)(functools.partial(
            _cmp_win_body, q_ref, kc_ref, vc_ref, kw2_ref, kw1_ref, kw0_ref, vw2_ref, vw1_ref, vw0_ref, ov_ref, gate_ref,
            ocw_ref, imp_ref, tq=tq, n_cmp=(c + 1) * chunk))


def _cmp_win_body(q_ref, kc_ref, vc_ref, kw2_ref, kw1_ref, kw0_ref, vw2_ref, vw1_ref, vw0_ref, ov_ref, gate_ref,
                  ocw_ref, imp_ref, *, tq, n_cmp):
    i = pl.program_id(1)
    r_heads = HEADS_PER_GROUP
    s0 = i * tq
    q4 = q_ref[...].reshape(r_heads * tq, HEAD_DIM)
    tok = s0 + lax.broadcasted_iota(jnp.int32, (tq, 1), 0)

    kwin = jnp.concatenate([kw2_ref[0], kw1_ref[0], kw0_ref[0]], axis=0)
    s_w = _dot_nt(q4, kwin)
    s_c = _dot_nt(q4, kc_ref[0, :n_cmp, :])
    vwin = jnp.concatenate([vw2_ref[0], vw1_ref[0], vw0_ref[0]], axis=0)
    wpos = s0 - 2 * tq + lax.broadcasted_iota(jnp.int32, (1, 3 * tq), 1)
    diff = tok - wpos
    bias_w = jnp.where((diff >= 0) & (diff < WINDOW) & (wpos >= 0), 0.0, NEG)
    p_w = jnp.concatenate([_softmax_numerators(s_w[r * tq:(r + 1) * tq, :], bias_w) for r in range(r_heads)], axis=0)
    res_w = _dot(p_w, jnp.concatenate([vwin, jnp.ones((3 * tq, HEAD_DIM), MXU_DTYPE)], axis=1))
    cmp_end = lax.broadcasted_iota(jnp.int32, (1, n_cmp), 1) * CMP_STRIDE + (CMP_BLOCK - 1)
    bias_c = jnp.where(cmp_end <= tok, 0.0, NEG)
    p_c = jnp.concatenate([_softmax_numerators(s_c[r * tq:(r + 1) * tq, :], bias_c) for r in range(r_heads)], axis=0)
    ones_c = jnp.ones((n_cmp, HEAD_DIM), MXU_DTYPE)
    res_c = _dot(p_c, jnp.concatenate([vc_ref[0, :n_cmp, :], ones_c, ov_ref[:n_cmp, :]], axis=1))
    o_c, imp = [], None
    for r in range(r_heads):
        rows = slice(r * tq, (r + 1) * tq)
        l = res_c[rows, HEAD_DIM:2 * HEAD_DIM]
        inv = jnp.where(l > 0.0, 1.0 / l, 0.0)
        o_c.append(res_c[rows, :HEAD_DIM] * inv)
        part = res_c[rows, 2 * HEAD_DIM:] * inv[:, :1]
        imp = part if imp is None else imp + part
    imp_ref[0] = imp

    gate = gate_ref[...]
    for r in range(r_heads):
        rows = slice(r * tq, (r + 1) * tq)
        g_cmp = gate[:, 3 * r:3 * r + 1]
        g_win = gate[:, 3 * r + 2:3 * r + 3]
        o_w = res_w[rows, :HEAD_DIM] / res_w[rows, HEAD_DIM:]
        ocw_ref[:, r * HEAD_DIM:(r + 1) * HEAD_DIM] = g_cmp * o_c[r] + g_win * o_w


def _overlap_matrix(n_cmp_rows, n_slc):
    cs = jnp.arange(n_cmp_rows)[:, None] * CMP_STRIDE
    ss = jnp.arange(n_slc)[None, :] * SEL_BLOCK
    ov = jnp.clip(jnp.minimum(cs + CMP_BLOCK, ss + SEL_BLOCK) - jnp.maximum(cs, ss), 0, None)
    return (ov.astype(F32) / CMP_BLOCK).astype(MXU_DTYPE)


def _cmp_win(qkv, kvc, gates, s):
    tq = min(s, 256)
    assert WINDOW <= 2 * tq
    g = N_KV_GROUPS
    n_cmp = kvc.shape[1]
    n_slc = s // SEL_BLOCK
    head = lambda base: [
        pl.BlockSpec((1, tq, HEAD_DIM), functools.partial(lambda a, i, d, base: (base + a, jnp.maximum(i - d, 0), 0), d=d, base=base))
        for d in (2, 1, 0)
    ]
    ocw, imp = pl.pallas_call(
        functools.partial(_cmp_win_kernel, tq=tq, chunk=min(n_cmp, MXU_WIDTH)),
        grid=(g, s // tq),
        in_specs=[
            pl.BlockSpec((HEADS_PER_GROUP, tq, HEAD_DIM), lambda a, i: (a, i, 0)),
            pl.BlockSpec((1, n_cmp, HEAD_DIM), lambda a, i: (a, 0, 0)),
            pl.BlockSpec((1, n_cmp, HEAD_DIM), lambda a, i: (g + a, 0, 0)),
            *head(N_HEADS + 2 * g),
            *head(N_HEADS + 3 * g),
            pl.BlockSpec((n_cmp, n_slc), lambda a, i: (0, 0)),
            pl.BlockSpec((tq, HEAD_DIM), lambda a, i: (i, a)),
        ],
        out_specs=[
            pl.BlockSpec((tq, HEADS_PER_GROUP * HEAD_DIM), lambda a, i: (i, a)),
            pl.BlockSpec((1, tq, n_slc), lambda a, i: (a, i, 0)),
        ],
        out_shape=[
            jax.ShapeDtypeStruct((s, N_HEADS * HEAD_DIM), F32),
            jax.ShapeDtypeStruct((g, s, n_slc), F32),
        ],
        compiler_params=_params("parallel", "parallel"),
        name="cmp_win",
    )(qkv, kvc, kvc, qkv, qkv, qkv, qkv, qkv, qkv, _overlap_matrix(n_cmp, n_slc), gates)
    return ocw, imp


def _topk_kernel(imp_ref, sb_ref, work_ref):
    tm, n_slc = work_ref.shape
    tok = pl.program_id(1) * tm + lax.broadcasted_iota(jnp.int32, (tm, 1), 0)
    blk = lax.broadcasted_iota(jnp.int32, (1, n_slc), 1)
    cur = tok >> SEL_SHIFT
    forced = (blk == cur) | (blk == 0)
    candidate = (blk <= cur) & jnp.logical_not(forced)
    n_pick = min(SEL_TOP, n_slc) - 2
    start = lambda: jnp.where(forced, REMOVED, jnp.where(blk <= cur, imp_ref[0], NEG))

    work_ref[...] = start()
    for _ in range(n_pick):
        w = work_ref[...]
        work_ref[...] = jnp.where(w == jnp.max(w, axis=-1, keepdims=True), REMOVED, w)
    taken = jnp.sum(jnp.where((work_ref[...] == REMOVED) & candidate, 1.0, 0.0))
    expected = jnp.sum(jnp.minimum(jnp.maximum(cur - 1, 0), n_pick).astype(F32))

    @pl.when(taken > expected)
    def _():
        blk_f = blk.astype(F32)
        work_ref[...] = start()
        for _ in range(n_pick):
            w = work_ref[...]
            best = jnp.max(w, axis=-1, keepdims=True)
            idx = jnp.min(jnp.where(w == best, blk_f, float(n_slc)), axis=-1, keepdims=True)
            work_ref[...] = jnp.where(blk_f == idx, REMOVED, w)

    sb_ref[0] = jnp.where(work_ref[...] == REMOVED, 0.0, NEG).astype(sb_ref.dtype)


def _topk_bias(imp):
    g, s, n_slc = imp.shape
    tm = min(s, 1024)
    spec = pl.BlockSpec((1, tm, n_slc), lambda a, i: (a, i, 0))
    return pl.pallas_call(
        _topk_kernel,
        grid=(g, s // tm),
        in_specs=[spec],
        out_specs=spec,
        out_shape=jax.ShapeDtypeStruct((g, s, n_slc), MXU_DTYPE),
        scratch_shapes=[pltpu.VMEM((tm, n_slc), F32)],
        compiler_params=_params("parallel", "parallel"),
        name="topk_bias",
    )(imp)


def _sel_kernel(q_ref, k_ref, v_ref, sb_ref, ocw_ref, gate_ref, o_ref, lhs_ref, s_ref, m_ref, acc_ref, *, tq, tk):
    i = pl.program_id(1)
    r_heads = HEADS_PER_GROUP
    bias_lanes = lhs_ref.shape[2] - HEAD_DIM
    blocks_per_tile = tk // SEL_BLOCK
    lane_tiles = tk // LANES
    n_tiles = (i * tq + tq - 1) // tk + 1
    tok = i * tq + lax.broadcasted_iota(jnp.int32, (tq, 1), 0)
    key_blk = lax.broadcasted_iota(jnp.int32, (tk, bias_lanes), 0) >> SEL_SHIFT
    lane = lax.broadcasted_iota(jnp.int32, (tk, bias_lanes), 1)
    key_off = lax.broadcasted_iota(jnp.int32, (1, tk), 1)

    for c in range(lhs_ref.shape[0]):
        for r in range(r_heads):
            rows = slice(r * tq, (r + 1) * tq)
            lhs_ref[c, rows, :HEAD_DIM] = q_ref[r]
            lhs_ref[c, rows, HEAD_DIM:] = sb_ref[0, :, c * bias_lanes:(c + 1) * bias_lanes]

    def produce(slot, j, parts=1):
        start = pl.multiple_of(j * tk, tk)
        first_blk = j * blocks_per_tile
        onehot = jnp.where(lane == first_blk % bias_lanes + key_blk, 1.0, 0.0).astype(MXU_DTYPE)
        k_ext = jnp.concatenate([k_ref[0, pl.ds(start, tk), :], onehot], axis=1)
        n = r_heads * tq // parts
        for h in range(parts):
            s_ref[slot, h * n:(h + 1) * n, :] = _dot_nt(lhs_ref[first_blk // bias_lanes, h * n:(h + 1) * n, :], k_ext)

    def consume(slot, j, diagonal, parts=1):
        start = pl.multiple_of(j * tk, tk)
        v_ext = jnp.concatenate([v_ref[0, pl.ds(start, tk), :], jnp.ones((tk, HEAD_DIM), MXU_DTYPE)], axis=1)
        p_rows, alphas = [], []
        for r in range(r_heads):
            rows = slice(r * tq, (r + 1) * tq)
            sm = s_ref[slot, rows, :]
            if diagonal:
                sm = jnp.where(j * tk + key_off <= tok, sm, NEG)
            m_prev = m_ref[rows, :]
            m_new = jnp.maximum(m_prev, jnp.max(sm, axis=-1, keepdims=True))
            m_ref[rows, :] = m_new
            p_rows.append(jnp.exp2(sm - jnp.concatenate([m_new] * lane_tiles, axis=1)).astype(MXU_DTYPE))
            alphas.append(jnp.exp2(m_prev - m_new))
        per = r_heads // parts
        for h in range(parts):
            pv = _dot(jnp.concatenate(p_rows[h * per:(h + 1) * per], axis=0), v_ext)
            for r in range(per):
                rows = slice((h * per + r) * tq, (h * per + r + 1) * tq)
                scale = jnp.concatenate([alphas[h * per + r]] * 2, axis=1)
                acc_ref[rows, :] = acc_ref[rows, :] * scale + pv[r * tq:(r + 1) * tq, :]

    produce(0, 0, parts=2)
    m_ref[...] = jnp.full(m_ref.shape, NEG, F32)
    acc_ref[...] = jnp.zeros(acc_ref.shape, F32)

    def body(pair, carry):
        j = 2 * pair
        consume(0, j, diagonal=False)
        produce(1, j + 1)
        consume(1, j + 1, diagonal=False)
        produce(0, j + 2)
        return carry

    n_pairs = (n_tiles - 1) // 2
    lax.fori_loop(0, n_pairs, body, 0)
    done = 2 * n_pairs

    @pl.when(n_tiles - done == 1)
    def _():
        consume(0, done, diagonal=True, parts=2)

    @pl.when(n_tiles - done == 2)
    def _():
        consume(0, done, diagonal=False)
        produce(1, done + 1)
        consume(1, done + 1, diagonal=True, parts=2)

    gate = gate_ref[...]
    for r in range(r_heads):
        rows = slice(r * tq, (r + 1) * tq)
        cols = slice(r * HEAD_DIM, (r + 1) * HEAD_DIM)
        o_s = acc_ref[rows, :HEAD_DIM] / acc_ref[rows, HEAD_DIM:]
        o_ref[:, cols] = (ocw_ref[:, cols] + gate[:, 3 * r + 1:3 * r + 2] * o_s).astype(o_ref.dtype)


def _sel_attention(qkv, sel_bias, ocw, gates, s):
    tq = min(s, 512)
    tk = min(s, 512)
    g = N_KV_GROUPS
    n_slc = s // SEL_BLOCK
    width = HEADS_PER_GROUP * HEAD_DIM
    rows = HEADS_PER_GROUP * tq
    bias_lanes = min(LANES, n_slc)
    assert n_slc % bias_lanes == 0 and bias_lanes % (tk // SEL_BLOCK) == 0 and tk % tq == 0
    return pl.pallas_call(
        functools.partial(_sel_kernel, tq=tq, tk=tk),
        grid=(g, s // tq),
        in_specs=[
            pl.BlockSpec((HEADS_PER_GROUP, tq, HEAD_DIM), lambda a, i: (a, i, 0)),
            pl.BlockSpec((1, s, HEAD_DIM), lambda a, i: (N_HEADS + a, 0, 0)),
            pl.BlockSpec((1, s, HEAD_DIM), lambda a, i: (N_HEADS + g + a, 0, 0)),
            pl.BlockSpec((1, tq, n_slc), lambda a, i: (a, i, 0)),
            pl.BlockSpec((tq, width), lambda a, i: (i, a)),
            pl.BlockSpec((tq, HEAD_DIM), lambda a, i: (i, a)),
        ],
        out_specs=pl.BlockSpec((tq, width), lambda a, i: (i, a)),
        out_shape=jax.ShapeDtypeStruct((s, N_HEADS * HEAD_DIM), MXU_DTYPE),
        scratch_shapes=[
            pltpu.VMEM((n_slc // bias_lanes, rows, HEAD_DIM + bias_lanes), MXU_DTYPE),
            pltpu.VMEM((2, rows, tk), F32),
            pltpu.VMEM((rows, HEAD_DIM), F32),
            pltpu.VMEM((rows, 2 * HEAD_DIM), F32),
        ],
        compiler_params=_params("parallel", "arbitrary"),
        name="sel_attention",
    )(qkv, qkv, qkv, sel_bias, ocw, gates)


def _merge_kernel(h_ref, cf_ref, at_ref, wga_ref, wgb_ref, wco_ref, wno_ref, o_ref):
    h = h_ref[...]
    y_a = _dot(cf_ref[...], wco_ref[...])
    y_b = _dot(at_ref[...], wno_ref[...])
    merged = jax.nn.sigmoid(_dot(h, wga_ref[...])) * y_a + jax.nn.sigmoid(_dot(h, wgb_ref[...])) * y_b
    o_ref[...] = merged.astype(o_ref.dtype)


def _merge(h, cf, attn, w_ga, w_gb, w_co, w_no):
    s, d = h.shape
    tm = min(s, 1024)
    tn = 512
    row = lambda width: pl.BlockSpec((tm, width), lambda i, j: (i, 0))
    col = lambda depth: pl.BlockSpec((depth, tn), lambda i, j: (0, j))
    return pl.pallas_call(
        _merge_kernel,
        grid=(s // tm, d // tn),
        in_specs=[row(d), row(CONV_WIDTH), row(N_HEADS * HEAD_DIM), col(d), col(d), col(CONV_WIDTH), col(N_HEADS * HEAD_DIM)],
        out_specs=pl.BlockSpec((tm, tn), lambda i, j: (i, j)),
        out_shape=jax.ShapeDtypeStruct((s, d), MXU_DTYPE),
        compiler_params=_params("parallel", "parallel"),
        name="merge",
    )(h, cf, attn, w_ga, w_gb, w_co, w_no)


def _out_proj_kernel(m_ref, w_ref, x_ref, gain_ref, g_ref, o_ref):
    y = _dot(m_ref[...], w_ref[...])
    o_ref[...] = x_ref[...] + g_ref[...] * _rms(y, gain_ref[...])


def _out_proj(merged, w_out, x, gain, g1):
    s, d = x.shape
    tm = min(s, 512)
    vec = pl.BlockSpec((1, d), lambda i: (0, 0))
    row = pl.BlockSpec((tm, d), lambda i: (i, 0))
    return pl.pallas_call(
        _out_proj_kernel,
        grid=(s // tm,),
        in_specs=[row, pl.BlockSpec((d, d), lambda i: (0, 0)), row, vec, vec],
        out_specs=row,
        out_shape=jax.ShapeDtypeStruct((s, d), F32),
        compiler_params=_params("parallel"),
        name="out_proj",
    )(merged, w_out, x, gain, g1)


def _mlp_kernel(x_ref, xn_ref, gin_ref, sc_ref, sh_ref, wu_ref, wd_ref, gout_ref, g_ref, o_ref, h0_ref, h1_ref, acc_ref,
                *, n_f):
    i = pl.program_id(0)
    f = pl.program_id(1)
    chunk = h0_ref.shape[0] // n_f

    def pre_norm(x):
        return (_rms(x, gin_ref[...]) * (1.0 + sc_ref[...]) + sh_ref[...]).astype(h0_ref.dtype)

    @pl.when((i == 0) & (f == 0))
    def _():
        h0_ref[...] = pre_norm(x_ref[...])

    @pl.when(f == 0)
    def _():
        acc_ref[...] = jnp.zeros(acc_ref.shape, F32)

    def step(h_ref, h_next_ref):
        r0 = pl.multiple_of(f * chunk, chunk)
        h_next_ref[pl.ds(r0, chunk), :] = pre_norm(xn_ref[pl.ds(r0, chunk), :])
        u = jnp.square(jnp.maximum(_dot(h_ref[...], wu_ref[...]), 0.0))
        acc_ref[...] += _dot(u.astype(MXU_DTYPE), wd_ref[...])

    pl.when(i % 2 == 0)(functools.partial(step, h0_ref, h1_ref))
    pl.when(i % 2 == 1)(functools.partial(step, h1_ref, h0_ref))

    @pl.when(f == pl.num_programs(1) - 1)
    def _():
        o_ref[...] = x_ref[...] + g_ref[...] * _rms(acc_ref[...], gout_ref[...])


def _mlp(x, gain_in, sc, sh, w_up, w_down, gain_out, g2):
    s, d = x.shape
    tm = min(s, 512)
    tf = 1024
    n_i, n_f = s // tm, D_FF // tf
    assert tm % (PACKED_ROWS * n_f) == 0
    vec = pl.BlockSpec((1, d), lambda i, f: (0, 0))
    row = pl.BlockSpec((tm, d), lambda i, f: (i, 0))
    next_row = pl.BlockSpec((tm, d), lambda i, f: (jnp.minimum(i + 1, n_i - 1), 0))
    return pl.pallas_call(
        functools.partial(_mlp_kernel, n_f=n_f),
        grid=(n_i, n_f),
        in_specs=[row, next_row, vec, vec, vec, pl.BlockSpec((d, tf), lambda i, f: (0, f)),
                  pl.BlockSpec((tf, d), lambda i, f: (f, 0)), vec, vec],
        out_specs=row,
        out_shape=jax.ShapeDtypeStruct((s, d), F32),
        scratch_shapes=[pltpu.VMEM((tm, d), MXU_DTYPE), pltpu.VMEM((tm, d), MXU_DTYPE), pltpu.VMEM((tm, d), F32)],
        compiler_params=_params("arbitrary", "arbitrary"),
        name="mlp",
    )(x, x, gain_in, sc, sh, w_up, w_down, gain_out, g2)


def _gate_weight(w_in_l):
    w = w_in_l[:, COL_GATE:COL_GA].reshape(D_MODEL, N_KV_GROUPS, GATES_PER_GROUP)
    w = jnp.pad(w, ((0, 0), (0, 0), (0, HEAD_DIM - GATES_PER_GROUP)))
    return w.reshape(D_MODEL, N_KV_GROUPS * HEAD_DIM).astype(MXU_DTYPE)


def kernel(x, c, positions, ada_w, ada_b, norm_gains, w_in, conv_w, w_conv_out, cmp_pe, cmp_w1, cmp_b1, cmp_w2, cmp_b2,
           w_nsa_out, w_out, w_mlp_up, w_mlp_down):
    b, s, d = x.shape
    assert b == 1 and d == D_MODEL
    depth = ada_w.shape[0]
    cast = lambda w: w.astype(MXU_DTYPE)
    xs = x.reshape(s, d)
    mod = _adaln(c, ada_w, ada_b)
    tables = _rope_tables(positions)
    ks_col = COL_KV + 2 * KV_WIDTH
    for l in range(depth):
        sh1, sc1, g1, sh2, sc2, g2 = [mod[l, k * d:(k + 1) * d].reshape(1, d) for k in range(6)]
        gains = [norm_gains[l, k].reshape(1, d) for k in range(4)]
        wl = w_in[l]
        h = _norm_mod(xs, gains[0], sc1, sh1)
        cf = _conv_proj(h, cast(wl[:, :COL_Q]), conv_w[l])
        w_qkv = cast(jnp.concatenate([wl[:, COL_Q:COL_KV], wl[:, ks_col:COL_GATE]], axis=1))
        qkv = _head_proj(h, w_qkv, tables, (0, 1, 2, 3, 4, 6), (0, 1, 2, 3), MXU_DTYPE)
        kv_raw = _head_proj(h, cast(wl[:, COL_KV:ks_col]), tables, (0,), (), F32)
        gates = _gate_proj(h, _gate_weight(wl))
        kvc = _compress(kv_raw, cmp_pe[l], cast(cmp_w1[l]), cmp_b1[l], cast(cmp_w2[l]), cmp_b2[l])
        ocw, imp = _cmp_win(qkv, kvc, gates, s)
        attn = _sel_attention(qkv, _topk_bias(imp), ocw, gates, s)
        merged = _merge(h, cf, attn, cast(wl[:, COL_GA:COL_GB]), cast(wl[:, COL_GB:]), cast(w_conv_out[l]), cast(w_nsa_out[l]))
        xs = _out_proj(merged, cast(w_out[l]), xs, gains[1], g1)
        xs = _mlp(xs, gains[2], sc2, sh2, cast(w_mlp_up[l]), cast(w_mlp_down[l]), gains[3], g2)
    return xs.reshape(b, s, d)
```

```python
import functools

import jax
import jax.numpy as jnp
from jax import lax
from jax.experimental import pallas as pl
from jax.experimental.pallas import tpu as pltpu

F32 = jnp.float32
MXU_DTYPE = jnp.bfloat16

D_MODEL = 2048
CONV_WIDTH = D_MODEL // 2
N_HEADS = 16
N_KV_GROUPS = 4
HEADS_PER_GROUP = N_HEADS // N_KV_GROUPS
HEAD_DIM = D_MODEL // N_HEADS
KV_WIDTH = N_KV_GROUPS * HEAD_DIM
ROPE_DIM = HEAD_DIM // 4
ROPE_THETA = 500000.0
CMP_BLOCK = 32
CMP_STRIDE = 16
CMP_HIDDEN = 2 * HEAD_DIM
SEL_BLOCK = 64
SEL_SHIFT = 6
SEL_TOP = 16
WINDOW = 512
D_FF = 4 * D_MODEL
NORM_EPS = 1e-6
NEG = -1e30
REMOVED = -3e38
EXP2_SCALE = HEAD_DIM ** -0.5 * 1.4426950408889634

COL_Q = 3 * CONV_WIDTH
COL_KV = COL_Q + N_HEADS * HEAD_DIM
COL_GATE = COL_KV + 6 * KV_WIDTH
COL_GA = COL_GATE + 3 * N_HEADS
COL_GB = COL_GA + D_MODEL
GATES_PER_GROUP = 3 * HEADS_PER_GROUP

VMEM_LIMIT_BYTES = 56 * 1024 * 1024
LANES = 128
SUBLANES = 8
PACKED_ROWS = 16
MXU_WIDTH = 256
assert HEAD_DIM == LANES


def _params(*semantics):
    return pltpu.CompilerParams(dimension_semantics=semantics, vmem_limit_bytes=VMEM_LIMIT_BYTES)


def _dot(a, b):
    return jnp.dot(a, b, preferred_element_type=F32)


def _dot_nt(a, b):
    return lax.dot_general(a, b, (((1,), (1,)), ((), ())), preferred_element_type=F32)


def _rms(y, gain):
    ms = jnp.mean(y * y, axis=-1, keepdims=True)
    return y * lax.rsqrt(ms + NORM_EPS) * gain


def _adaln_kernel(c_ref, w_ref, b_ref, o_ref):
    c = c_ref[...]
    d, tn = w_ref.shape[1:]
    act = (c * jax.nn.sigmoid(c)).reshape(d // SUBLANES, SUBLANES, LANES)
    for t in range(tn // LANES):
        cols = slice(t * LANES, (t + 1) * LANES)
        part = jnp.sum(w_ref[0, :, cols].reshape(d // SUBLANES, SUBLANES, LANES) * act, axis=0)
        o_ref[0, :, cols] = jnp.sum(part, axis=0, keepdims=True) + b_ref[0, :, cols]


def _adaln(c, ada_w, ada_b):
    depth, d, n = ada_w.shape
    tn = 1024
    c_rows = jnp.broadcast_to(c.reshape(d, 1), (d, LANES))
    vec = pl.BlockSpec((1, 1, tn), lambda l, j: (l, 0, j))
    out = pl.pallas_call(
        _adaln_kernel,
        grid=(depth, n // tn),
        in_specs=[pl.BlockSpec((d, LANES), lambda l, j: (0, 0)), pl.BlockSpec((1, d, tn), lambda l, j: (l, 0, j)), vec],
        out_specs=vec,
        out_shape=jax.ShapeDtypeStruct((depth, 1, n), F32),
        compiler_params=_params("parallel", "parallel"),
        name="adaln",
    )(c_rows, ada_w, ada_b.reshape(depth, 1, n))
    return out[:, 0, :]


def _rope_table_kernel(pos_ref, freq_ref, c_ref, s1_ref, s2_ref):
    half = ROPE_DIM // 2
    ang = pos_ref[...].astype(F32) * freq_ref[...]
    lane = lax.broadcasted_iota(jnp.int32, ang.shape, 1)
    cos = jnp.cos(ang)
    sin = jnp.sin(ang)
    c_ref[...] = jnp.where(lane < ROPE_DIM, cos, 1.0)
    s1_ref[...] = jnp.where(lane < half, -sin, 0.0)
    s2_ref[...] = jnp.where((lane >= half) & (lane < ROPE_DIM), sin, 0.0)


def _rope_tables(positions):
    s = positions.shape[-1]
    tm = min(s, 2048)
    half = ROPE_DIM // 2
    inv_freq = ROPE_THETA ** (-jnp.arange(0, ROPE_DIM, 2, dtype=F32) / ROPE_DIM)
    freq = jnp.concatenate([inv_freq, inv_freq, jnp.zeros((HEAD_DIM - 2 * half,), F32)]).reshape(1, HEAD_DIM)
    spec = pl.BlockSpec((tm, HEAD_DIM), lambda i: (i, 0))
    shape = jax.ShapeDtypeStruct((s, HEAD_DIM), F32)
    return pl.pallas_call(
        _rope_table_kernel,
        grid=(s // tm,),
        in_specs=[pl.BlockSpec((tm, 1), lambda i: (i, 0)), pl.BlockSpec((1, HEAD_DIM), lambda i: (0, 0))],
        out_specs=[spec, spec, spec],
        out_shape=[shape, shape, shape],
        compiler_params=_params("parallel"),
        name="rope_tables",
    )(positions.reshape(s, 1), freq)


def _rope(x, c, s1, s2):
    half = ROPE_DIM // 2
    return x * c + pltpu.roll(x, HEAD_DIM - half, axis=1) * s1 + pltpu.roll(x, half, axis=1) * s2


def _norm_mod_kernel(x_ref, g_ref, sc_ref, sh_ref, o_ref):
    y = _rms(x_ref[...], g_ref[...])
    o_ref[...] = (y * (1.0 + sc_ref[...]) + sh_ref[...]).astype(o_ref.dtype)


def _norm_mod(x, gain, sc, sh):
    s, d = x.shape
    tm = min(s, 512)
    vec = pl.BlockSpec((1, d), lambda i: (0, 0))
    return pl.pallas_call(
        _norm_mod_kernel,
        grid=(s // tm,),
        in_specs=[pl.BlockSpec((tm, d), lambda i: (i, 0)), vec, vec, vec],
        out_specs=pl.BlockSpec((tm, d), lambda i: (i, 0)),
        out_shape=jax.ShapeDtypeStruct((s, d), MXU_DTYPE),
        compiler_params=_params("parallel"),
        name="norm_mod",
    )(x, gain, sc, sh)


def _conv_proj_kernel(h_ref, wb_ref, wc_ref, wx_ref, cw_ref, o_ref, carry_ref):
    i = pl.program_id(0)
    j = pl.program_id(1)
    tm = h_ref.shape[0]
    h = h_ref[...]
    u = _dot(h, wc_ref[...]) * _dot(h, wx_ref[...])

    @pl.when(i == 0)
    def _():
        carry_ref[j] = jnp.zeros(carry_ref.shape[1:], F32)

    prev = carry_ref[j]
    carry_ref[j] = u[tm - SUBLANES:, :]
    row = lax.broadcasted_iota(jnp.int32, u.shape, 0)
    last, before_last = prev[SUBLANES - 1:, :], prev[SUBLANES - 2:SUBLANES - 1, :]
    u1 = jnp.where(row == 0, last, pltpu.roll(u, 1, axis=0))
    u2 = jnp.where(row == 0, before_last, jnp.where(row == 1, last, pltpu.roll(u, 2, axis=0)))
    cw = cw_ref[...]
    z = cw[2:3, :] * u + cw[1:2, :] * u1 + cw[0:1, :] * u2
    o_ref[...] = (_dot(h, wb_ref[...]) * z).astype(o_ref.dtype)


def _conv_proj(h, w_conv, conv_w):
    s, d = h.shape
    tm = min(s, 1024)
    tn = 512
    nb = CONV_WIDTH // tn
    return pl.pallas_call(
        _conv_proj_kernel,
        grid=(s // tm, nb),
        in_specs=[
            pl.BlockSpec((tm, d), lambda i, j: (i, 0)),
            pl.BlockSpec((d, tn), lambda i, j: (0, j)),
            pl.BlockSpec((d, tn), lambda i, j: (0, j + nb)),
            pl.BlockSpec((d, tn), lambda i, j: (0, j + 2 * nb)),
            pl.BlockSpec((3, tn), lambda i, j: (0, j)),
        ],
        out_specs=pl.BlockSpec((tm, tn), lambda i, j: (i, j)),
        out_shape=jax.ShapeDtypeStruct((s, CONV_WIDTH), MXU_DTYPE),
        scratch_shapes=[pltpu.VMEM((nb, SUBLANES, tn), F32)],
        compiler_params=_params("arbitrary", "arbitrary"),
        name="conv_proj",
    )(h, w_conv, w_conv, w_conv, conv_w)


def _head_proj_kernel(h_ref, w_ref, c_ref, s1_ref, s2_ref, o_ref, *, rope_groups, query_groups):
    j = pl.program_id(1)
    groups_per_step = o_ref.shape[0] // HEADS_PER_GROUP
    width = HEADS_PER_GROUP * HEAD_DIM
    any_of = lambda group, members: functools.reduce(jnp.logical_or, [group == r for r in members], False)
    for k in range(groups_per_step):
        acc = _dot(h_ref[...], w_ref[:, k * width:(k + 1) * width])
        is_rope = any_of(j * groups_per_step + k, rope_groups)
        q_scale = jnp.where(any_of(j * groups_per_step + k, query_groups), EXP2_SCALE, 1.0)
        c = jnp.where(is_rope, c_ref[...], 1.0) * q_scale
        s1 = jnp.where(is_rope, s1_ref[...], 0.0) * q_scale
        s2 = jnp.where(is_rope, s2_ref[...], 0.0) * q_scale
        for r in range(HEADS_PER_GROUP):
            head = _rope(acc[:, r * HEAD_DIM:(r + 1) * HEAD_DIM], c, s1, s2)
            o_ref[k * HEADS_PER_GROUP + r] = head.astype(o_ref.dtype)


def _head_proj(h, w, tables, rope_groups, query_groups, out_dtype):
    s, d = h.shape
    tm = min(s, 512)
    tn = min(w.shape[1], 4 * HEADS_PER_GROUP * HEAD_DIM)
    heads_per_step = tn // HEAD_DIM
    tab = pl.BlockSpec((tm, HEAD_DIM), lambda i, j: (i, 0))
    return pl.pallas_call(
        functools.partial(_head_proj_kernel, rope_groups=rope_groups, query_groups=query_groups),
        grid=(s // tm, w.shape[1] // tn),
        in_specs=[pl.BlockSpec((tm, d), lambda i, j: (i, 0)), pl.BlockSpec((d, tn), lambda i, j: (0, j)), tab, tab, tab],
        out_specs=pl.BlockSpec((heads_per_step, tm, HEAD_DIM), lambda i, j: (j, i, 0)),
        out_shape=jax.ShapeDtypeStruct((w.shape[1] // HEAD_DIM, s, HEAD_DIM), out_dtype),
        compiler_params=_params("parallel", "parallel"),
        name="head_proj",
    )(h, w, *tables)


def _gate_proj_kernel(h_ref, w_ref, o_ref):
    o_ref[...] = jax.nn.sigmoid(_dot(h_ref[...], w_ref[...]))


def _gate_proj(h, w_gate):
    s, d = h.shape
    tm = min(s, 1024)
    n = w_gate.shape[1]
    return pl.pallas_call(
        _gate_proj_kernel,
        grid=(s // tm,),
        in_specs=[pl.BlockSpec((tm, d), lambda i: (i, 0)), pl.BlockSpec((d, n), lambda i: (0, 0))],
        out_specs=pl.BlockSpec((tm, n), lambda i: (i, 0)),
        out_shape=jax.ShapeDtypeStruct((s, n), F32),
        compiler_params=_params("parallel"),
        name="gate_proj",
    )(h, w_gate)


def _compress_kernel(x_ref, pe_ref, w1_ref, b1_ref, w2_ref, b2_ref, o_ref):
    n = o_ref.shape[1]
    k_lo = CMP_STRIDE * HEAD_DIM
    pe = pe_ref[0]
    lo, hi = [], []
    for l in range(CMP_STRIDE):
        rows = x_ref[0, pl.ds(l, n, stride=CMP_STRIDE), :]
        lo.append((rows + pe[l:l + 1, :]).astype(MXU_DTYPE))
        hi.append((rows + pe[CMP_STRIDE + l:CMP_STRIDE + l + 1, :]).astype(MXU_DTYPE))
    t_lo = _dot(jnp.concatenate(lo, axis=1), w1_ref[0, :k_lo, :])
    t_hi = _dot(jnp.concatenate(hi, axis=1), w1_ref[0, k_lo:, :])
    hid = t_lo + pltpu.roll(t_hi, n - 1, axis=0) + b1_ref[0]
    out = _dot(jax.nn.gelu(hid).astype(MXU_DTYPE), w2_ref[0]) + b2_ref[0]
    row = lax.broadcasted_iota(jnp.int32, out.shape, 0)
    o_ref[0] = jnp.where(row == n - 1, 0.0, out).astype(o_ref.dtype)


def _compress(kv_raw, pe, w1, b1, w2, b2):
    nh, s, dh = kv_raw.shape
    n = s // CMP_STRIDE
    g = N_KV_GROUPS
    return pl.pallas_call(
        _compress_kernel,
        grid=(nh,),
        in_specs=[
            pl.BlockSpec((1, s, dh), lambda a: (a, 0, 0)),
            pl.BlockSpec((1, CMP_BLOCK, dh), lambda a: (a // g, 0, 0)),
            pl.BlockSpec((1, CMP_BLOCK * dh, CMP_HIDDEN), lambda a: (a // g, 0, 0)),
            pl.BlockSpec((1, 1, CMP_HIDDEN), lambda a: (a // g, 0, 0)),
            pl.BlockSpec((1, CMP_HIDDEN, dh), lambda a: (a // g, 0, 0)),
            pl.BlockSpec((1, 1, dh), lambda a: (a // g, 0, 0)),
        ],
        out_specs=pl.BlockSpec((1, n, dh), lambda a: (a, 0, 0)),
        out_shape=jax.ShapeDtypeStruct((nh, n, dh), MXU_DTYPE),
        compiler_params=_params("parallel"),
        name="compress",
    )(kv_raw, pe, w1, b1.reshape(2, 1, CMP_HIDDEN), w2, b2.reshape(2, 1, dh))


def _softmax_numerators(s, bias, dtype=None):
    sm = s + bias
    m = jnp.max(sm, axis=-1, keepdims=True)
    m = jnp.where(m < 0.5 * NEG, 0.0, m)
    return jnp.exp2(sm - m).astype(MXU_DTYPE if dtype is None else dtype)


def _cmp_win_kernel(q_ref, kc_ref, vc_ref, kw2_ref, kw1_ref, kw0_ref, vw2_ref, vw1_ref, vw0_ref, ov_ref, band_ref, gate_ref,
                    ocw_ref, imp_ref, *, tq, chunk):
    n_cmp = kc_ref.shape[1]
    last_visible = ((pl.program_id(1) + 1) * tq - CMP_BLOCK) // CMP_STRIDE
    for c in range(n_cmp // chunk):
        pl.when(jnp.maximum(last_visible, 0) // chunk == c)(functools.partial(
            _cmp_win_body, q_ref, kc_ref, vc_ref, kw2_ref, kw1_ref, kw0_ref, vw2_ref, vw1_ref, vw0_ref, ov_ref, band_ref, gate_ref,
            ocw_ref, imp_ref, tq=tq, n_cmp=(c + 1) * chunk))


def _cmp_win_body(q_ref, kc_ref, vc_ref, kw2_ref, kw1_ref, kw0_ref, vw2_ref, vw1_ref, vw0_ref, ov_ref, band_ref, gate_ref,
                  ocw_ref, imp_ref, *, tq, n_cmp):
    i = pl.program_id(1)
    r_heads = HEADS_PER_GROUP
    s0 = i * tq
    q4 = q_ref[...].reshape(r_heads * tq, HEAD_DIM)
    tok = s0 + lax.broadcasted_iota(jnp.int32, (tq, 1), 0)

    kwin = jnp.concatenate([kw2_ref[0], kw1_ref[0], kw0_ref[0]], axis=0)
    s_w = _dot_nt(q4, kwin)
    s_c = _dot_nt(q4, kc_ref[0, :n_cmp, :])
    vwin = jnp.concatenate([vw2_ref[0], vw1_ref[0], vw0_ref[0]], axis=0)
    wpos = s0 - 2 * tq + lax.broadcasted_iota(jnp.int32, (1, 3 * tq), 1)
    bias_w = jnp.where(wpos >= 0, band_ref[...], NEG)
    p_w = jnp.concatenate([_softmax_numerators(s_w[r * tq:(r + 1) * tq, :], bias_w) for r in range(r_heads)], axis=0)
    res_w = _dot(p_w, jnp.concatenate([vwin, jnp.ones((3 * tq, HEAD_DIM), MXU_DTYPE)], axis=1))
    cmp_end = lax.broadcasted_iota(jnp.int32, (1, n_cmp), 1) * CMP_STRIDE + (CMP_BLOCK - 1)
    bias_c = jnp.where(cmp_end <= tok, 0.0, NEG)
    p_rows, p_sum = [], None
    for r in range(r_heads):
        p = _softmax_numerators(s_c[r * tq:(r + 1) * tq, :], bias_c, F32)
        l = jnp.sum(p, axis=-1, keepdims=True)
        p = p * jnp.where(l > 0.0, 1.0 / l, 0.0)
        p_rows.append(p.astype(MXU_DTYPE))
        p_sum = p if p_sum is None else p_sum + p
    res_c = _dot(jnp.concatenate(p_rows, axis=0), vc_ref[0, :n_cmp, :])
    o_c = [res_c[r * tq:(r + 1) * tq, :] for r in range(r_heads)]
    imp_ref[0] = _dot(p_sum.astype(MXU_DTYPE), ov_ref[:n_cmp, :])

    gate = gate_ref[...]
    for r in range(r_heads):
        rows = slice(r * tq, (r + 1) * tq)
        g_cmp = gate[:, 3 * r:3 * r + 1]
        g_win = gate[:, 3 * r + 2:3 * r + 3]
        o_w = res_w[rows, :HEAD_DIM] / res_w[rows, HEAD_DIM:]
        ocw_ref[:, r * HEAD_DIM:(r + 1) * HEAD_DIM] = g_cmp * o_c[r] + g_win * o_w


def _overlap_matrix(n_cmp_rows, n_slc):
    cs = jnp.arange(n_cmp_rows)[:, None] * CMP_STRIDE
    ss = jnp.arange(n_slc)[None, :] * SEL_BLOCK
    ov = jnp.clip(jnp.minimum(cs + CMP_BLOCK, ss + SEL_BLOCK) - jnp.maximum(cs, ss), 0, None)
    return (ov.astype(F32) / CMP_BLOCK).astype(MXU_DTYPE)


def _window_band(tq):
    diff = jnp.arange(tq)[:, None] + 2 * tq - jnp.arange(3 * tq)[None, :]
    return jnp.where((diff >= 0) & (diff < WINDOW), 0.0, NEG).astype(F32)


def _cmp_win(qkv, kvc, gates, s):
    tq = min(s, 256)
    assert WINDOW <= 2 * tq
    g = N_KV_GROUPS
    n_cmp = kvc.shape[1]
    n_slc = s // SEL_BLOCK
    head = lambda base: [
        pl.BlockSpec((1, tq, HEAD_DIM), functools.partial(lambda a, i, d, base: (base + a, jnp.maximum(i - d, 0), 0), d=d, base=base))
        for d in (2, 1, 0)
    ]
    ocw, imp = pl.pallas_call(
        functools.partial(_cmp_win_kernel, tq=tq, chunk=min(n_cmp, MXU_WIDTH)),
        grid=(g, s // tq),
        in_specs=[
            pl.BlockSpec((HEADS_PER_GROUP, tq, HEAD_DIM), lambda a, i: (a, i, 0)),
            pl.BlockSpec((1, n_cmp, HEAD_DIM), lambda a, i: (a, 0, 0)),
            pl.BlockSpec((1, n_cmp, HEAD_DIM), lambda a, i: (g + a, 0, 0)),
            *head(N_HEADS + 2 * g),
            *head(N_HEADS + 3 * g),
            pl.BlockSpec((n_cmp, n_slc), lambda a, i: (0, 0)),
            pl.BlockSpec((tq, 3 * tq), lambda a, i: (0, 0)),
            pl.BlockSpec((tq, HEAD_DIM), lambda a, i: (i, a)),
        ],
        out_specs=[
            pl.BlockSpec((tq, HEADS_PER_GROUP * HEAD_DIM), lambda a, i: (i, a)),
            pl.BlockSpec((1, tq, n_slc), lambda a, i: (a, i, 0)),
        ],
        out_shape=[
            jax.ShapeDtypeStruct((s, N_HEADS * HEAD_DIM), F32),
            jax.ShapeDtypeStruct((g, s, n_slc), F32),
        ],
        compiler_params=_params("parallel", "parallel"),
        name="cmp_win",
    )(qkv, kvc, kvc, qkv, qkv, qkv, qkv, qkv, qkv, _overlap_matrix(n_cmp, n_slc), _window_band(tq), gates)
    return ocw, imp


def _topk_kernel(imp_ref, sb_ref, work_ref):
    tm, n_slc = work_ref.shape
    tok = pl.program_id(1) * tm + lax.broadcasted_iota(jnp.int32, (tm, 1), 0)
    blk = lax.broadcasted_iota(jnp.int32, (1, n_slc), 1)
    cur = tok >> SEL_SHIFT
    forced = (blk == cur) | (blk == 0)
    candidate = (blk <= cur) & jnp.logical_not(forced)
    n_pick = min(SEL_TOP, n_slc) - 2
    start = lambda: jnp.where(forced, REMOVED, jnp.where(blk <= cur, imp_ref[0], NEG))

    work_ref[...] = start()
    for _ in range(n_pick):
        w = work_ref[...]
        work_ref[...] = jnp.where(w == jnp.max(w, axis=-1, keepdims=True), REMOVED, w)
    taken = jnp.sum(jnp.where((work_ref[...] == REMOVED) & candidate, 1.0, 0.0))
    expected = jnp.sum(jnp.minimum(jnp.maximum(cur - 1, 0), n_pick).astype(F32))

    @pl.when(taken > expected)
    def _():
        blk_f = blk.astype(F32)
        work_ref[...] = start()
        for _ in range(n_pick):
            w = work_ref[...]
            best = jnp.max(w, axis=-1, keepdims=True)
            idx = jnp.min(jnp.where(w == best, blk_f, float(n_slc)), axis=-1, keepdims=True)
            work_ref[...] = jnp.where(blk_f == idx, REMOVED, w)

    sb_ref[0] = jnp.where(work_ref[...] == REMOVED, 0.0, NEG).astype(sb_ref.dtype)


def _topk_bias(imp):
    g, s, n_slc = imp.shape
    tm = min(s, 1024)
    spec = pl.BlockSpec((1, tm, n_slc), lambda a, i: (a, i, 0))
    return pl.pallas_call(
        _topk_kernel,
        grid=(g, s // tm),
        in_specs=[spec],
        out_specs=spec,
        out_shape=jax.ShapeDtypeStruct((g, s, n_slc), MXU_DTYPE),
        scratch_shapes=[pltpu.VMEM((tm, n_slc), F32)],
        compiler_params=_params("parallel", "parallel"),
        name="topk_bias",
    )(imp)


def _sel_kernel(q_ref, k_ref, v_ref, sb_ref, ocw_ref, gate_ref, o_ref, lhs_ref, s_ref, m_ref, acc_ref, *, tq, tk):
    i = pl.program_id(1)
    r_heads = HEADS_PER_GROUP
    bias_lanes = lhs_ref.shape[2] - HEAD_DIM
    blocks_per_tile = tk // SEL_BLOCK
    lane_tiles = tk // LANES
    n_tiles = (i * tq + tq - 1) // tk + 1
    tok = i * tq + lax.broadcasted_iota(jnp.int32, (tq, 1), 0)
    key_blk = lax.broadcasted_iota(jnp.int32, (tk, bias_lanes), 0) >> SEL_SHIFT
    lane = lax.broadcasted_iota(jnp.int32, (tk, bias_lanes), 1)
    key_off = lax.broadcasted_iota(jnp.int32, (1, tk), 1)

    for c in range(lhs_ref.shape[0]):
        for r in range(r_heads):
            rows = slice(r * tq, (r + 1) * tq)
            lhs_ref[c, rows, :HEAD_DIM] = q_ref[r]
            lhs_ref[c, rows, HEAD_DIM:] = sb_ref[0, :, c * bias_lanes:(c + 1) * bias_lanes]

    def produce(slot, j, parts=1):
        start = pl.multiple_of(j * tk, tk)
        first_blk = j * blocks_per_tile
        onehot = jnp.where(lane == first_blk % bias_lanes + key_blk, 1.0, 0.0).astype(MXU_DTYPE)
        k_ext = jnp.concatenate([k_ref[0, pl.ds(start, tk), :], onehot], axis=1)
        n = r_heads * tq // parts
        for h in range(parts):
            s_ref[slot, h * n:(h + 1) * n, :] = _dot_nt(lhs_ref[first_blk // bias_lanes, h * n:(h + 1) * n, :], k_ext)

    def consume(slot, j, diagonal, parts=1):
        start = pl.multiple_of(j * tk, tk)
        v_ext = jnp.concatenate([v_ref[0, pl.ds(start, tk), :], jnp.ones((tk, HEAD_DIM), MXU_DTYPE)], axis=1)
        p_rows, alphas = [], []
        for r in range(r_heads):
            rows = slice(r * tq, (r + 1) * tq)
            sm = s_ref[slot, rows, :]
            if diagonal:
                sm = jnp.where(j * tk + key_off <= tok, sm, NEG)
            m_prev = m_ref[rows, :]
            m_new = jnp.maximum(m_prev, jnp.max(sm, axis=-1, keepdims=True))
            m_ref[rows, :] = m_new
            p_rows.append(jnp.exp2(sm - jnp.concatenate([m_new] * lane_tiles, axis=1)).astype(MXU_DTYPE))
            alphas.append(jnp.exp2(m_prev - m_new))
        per = r_heads // parts
        for h in range(parts):
            pv = _dot(jnp.concatenate(p_rows[h * per:(h + 1) * per], axis=0), v_ext)
            for r in range(per):
                rows = slice((h * per + r) * tq, (h * per + r + 1) * tq)
                scale = jnp.concatenate([alphas[h * per + r]] * 2, axis=1)
                acc_ref[rows, :] = acc_ref[rows, :] * scale + pv[r * tq:(r + 1) * tq, :]

    produce(0, 0, parts=2)
    m_ref[...] = jnp.full(m_ref.shape, NEG, F32)
    acc_ref[...] = jnp.zeros(acc_ref.shape, F32)

    def body(pair, carry):
        j = 2 * pair
        consume(0, j, diagonal=False)
        produce(1, j + 1)
        consume(1, j + 1, diagonal=False)
        produce(0, j + 2)
        return carry

    n_pairs = (n_tiles - 1) // 2
    lax.fori_loop(0, n_pairs, body, 0)
    done = 2 * n_pairs

    @pl.when(n_tiles - done == 1)
    def _():
        consume(0, done, diagonal=True, parts=2)

    @pl.when(n_tiles - done == 2)
    def _():
        consume(0, done, diagonal=False)
        produce(1, done + 1)
        consume(1, done + 1, diagonal=True, parts=2)

    gate = gate_ref[...]
    for r in range(r_heads):
        rows = slice(r * tq, (r + 1) * tq)
        cols = slice(r * HEAD_DIM, (r + 1) * HEAD_DIM)
        o_s = acc_ref[rows, :HEAD_DIM] / acc_ref[rows, HEAD_DIM:]
        o_ref[:, cols] = (ocw_ref[:, cols] + gate[:, 3 * r + 1:3 * r + 2] * o_s).astype(o_ref.dtype)


def _sel_attention(qkv, sel_bias, ocw, gates, s):
    tq = min(s, 512)
    tk = min(s, 512)
    g = N_KV_GROUPS
    n_slc = s // SEL_BLOCK
    width = HEADS_PER_GROUP * HEAD_DIM
    rows = HEADS_PER_GROUP * tq
    bias_lanes = min(LANES, n_slc)
    assert n_slc % bias_lanes == 0 and bias_lanes % (tk // SEL_BLOCK) == 0 and tk % tq == 0
    return pl.pallas_call(
        functools.partial(_sel_kernel, tq=tq, tk=tk),
        grid=(g, s // tq),
        in_specs=[
            pl.BlockSpec((HEADS_PER_GROUP, tq, HEAD_DIM), lambda a, i: (a, i, 0)),
            pl.BlockSpec((1, s, HEAD_DIM), lambda a, i: (N_HEADS + a, 0, 0)),
            pl.BlockSpec((1, s, HEAD_DIM), lambda a, i: (N_HEADS + g + a, 0, 0)),
            pl.BlockSpec((1, tq, n_slc), lambda a, i: (a, i, 0)),
            pl.BlockSpec((tq, width), lambda a, i: (i, a)),
            pl.BlockSpec((tq, HEAD_DIM), lambda a, i: (i, a)),
        ],
        out_specs=pl.BlockSpec((tq, width), lambda a, i: (i, a)),
        out_shape=jax.ShapeDtypeStruct((s, N_HEADS * HEAD_DIM), MXU_DTYPE),
        scratch_shapes=[
            pltpu.VMEM((n_slc // bias_lanes, rows, HEAD_DIM + bias_lanes), MXU_DTYPE),
            pltpu.VMEM((2, rows, tk), F32),
            pltpu.VMEM((rows, HEAD_DIM), F32),
            pltpu.VMEM((rows, 2 * HEAD_DIM), F32),
        ],
        compiler_params=_params("parallel", "arbitrary"),
        name="sel_attention",
    )(qkv, qkv, qkv, sel_bias, ocw, gates)


def _merge_kernel(h_ref, cf_ref, at_ref, wga_ref, wgb_ref, wco_ref, wno_ref, o_ref):
    h = h_ref[...]
    y_a = _dot(cf_ref[...], wco_ref[...])
    y_b = _dot(at_ref[...], wno_ref[...])
    merged = jax.nn.sigmoid(_dot(h, wga_ref[...])) * y_a + jax.nn.sigmoid(_dot(h, wgb_ref[...])) * y_b
    o_ref[...] = merged.astype(o_ref.dtype)


def _merge(h, cf, attn, w_ga, w_gb, w_co, w_no):
    s, d = h.shape
    tm = min(s, 1024)
    tn = 512
    row = lambda width: pl.BlockSpec((tm, width), lambda i, j: (i, 0))
    col = lambda depth: pl.BlockSpec((depth, tn), lambda i, j: (0, j))
    return pl.pallas_call(
        _merge_kernel,
        grid=(s // tm, d // tn),
        in_specs=[row(d), row(CONV_WIDTH), row(N_HEADS * HEAD_DIM), col(d), col(d), col(CONV_WIDTH), col(N_HEADS * HEAD_DIM)],
        out_specs=pl.BlockSpec((tm, tn), lambda i, j: (i, j)),
        out_shape=jax.ShapeDtypeStruct((s, d), MXU_DTYPE),
        compiler_params=_params("parallel", "parallel"),
        name="merge",
    )(h, cf, attn, w_ga, w_gb, w_co, w_no)


def _out_proj_kernel(m_ref, w_ref, x_ref, gain_ref, g_ref, o_ref):
    y = _dot(m_ref[...], w_ref[...])
    o_ref[...] = x_ref[...] + g_ref[...] * _rms(y, gain_ref[...])


def _out_proj(merged, w_out, x, gain, g1):
    s, d = x.shape
    tm = min(s, 512)
    vec = pl.BlockSpec((1, d), lambda i: (0, 0))
    row = pl.BlockSpec((tm, d), lambda i: (i, 0))
    return pl.pallas_call(
        _out_proj_kernel,
        grid=(s // tm,),
        in_specs=[row, pl.BlockSpec((d, d), lambda i: (0, 0)), row, vec, vec],
        out_specs=row,
        out_shape=jax.ShapeDtypeStruct((s, d), F32),
        compiler_params=_params("parallel"),
        name="out_proj",
    )(merged, w_out, x, gain, g1)


def _mlp_kernel(x_ref, xn_ref, gin_ref, sc_ref, sh_ref, wu_ref, wd_ref, gout_ref, g_ref, o_ref, h0_ref, h1_ref, acc_ref,
                *, n_f):
    i = pl.program_id(0)
    f = pl.program_id(1)
    chunk = h0_ref.shape[0] // n_f

    def pre_norm(x):
        return (_rms(x, gin_ref[...]) * (1.0 + sc_ref[...]) + sh_ref[...]).astype(h0_ref.dtype)

    @pl.when((i == 0) & (f == 0))
    def _():
        h0_ref[...] = pre_norm(x_ref[...])

    @pl.when(f == 0)
    def _():
        acc_ref[...] = jnp.zeros(acc_ref.shape, F32)

    def step(h_ref, h_next_ref):
        r0 = pl.multiple_of(f * chunk, chunk)
        h_next_ref[pl.ds(r0, chunk), :] = pre_norm(xn_ref[pl.ds(r0, chunk), :])
        u = jnp.square(jnp.maximum(_dot(h_ref[...], wu_ref[...]), 0.0))
        acc_ref[...] += _dot(u.astype(MXU_DTYPE), wd_ref[...])

    pl.when(i % 2 == 0)(functools.partial(step, h0_ref, h1_ref))
    pl.when(i % 2 == 1)(functools.partial(step, h1_ref, h0_ref))

    @pl.when(f == pl.num_programs(1) - 1)
    def _():
        o_ref[...] = x_ref[...] + g_ref[...] * _rms(acc_ref[...], gout_ref[...])


def _mlp(x, gain_in, sc, sh, w_up, w_down, gain_out, g2):
    s, d = x.shape
    tm = min(s, 512)
    tf = 1024
    n_i, n_f = s // tm, D_FF // tf
    assert tm % (PACKED_ROWS * n_f) == 0
    vec = pl.BlockSpec((1, d), lambda i, f: (0, 0))
    row = pl.BlockSpec((tm, d), lambda i, f: (i, 0))
    next_row = pl.BlockSpec((tm, d), lambda i, f: (jnp.minimum(i + 1, n_i - 1), 0))
    return pl.pallas_call(
        functools.partial(_mlp_kernel, n_f=n_f),
        grid=(n_i, n_f),
        in_specs=[row, next_row, vec, vec, vec, pl.BlockSpec((d, tf), lambda i, f: (0, f)),
                  pl.BlockSpec((tf, d), lambda i, f: (f, 0)), vec, vec],
        out_specs=row,
        out_shape=jax.ShapeDtypeStruct((s, d), F32),
        scratch_shapes=[pltpu.VMEM((tm, d), MXU_DTYPE), pltpu.VMEM((tm, d), MXU_DTYPE), pltpu.VMEM((tm, d), F32)],
        compiler_params=_params("arbitrary", "arbitrary"),
        name="mlp",
    )(x, x, gain_in, sc, sh, w_up, w_down, gain_out, g2)


def _gate_weight(w_in_l):
    w = w_in_l[:, COL_GATE:COL_GA].reshape(D_MODEL, N_KV_GROUPS, GATES_PER_GROUP)
    w = jnp.pad(w, ((0, 0), (0, 0), (0, HEAD_DIM - GATES_PER_GROUP)))
    return w.reshape(D_MODEL, N_KV_GROUPS * HEAD_DIM).astype(MXU_DTYPE)


def kernel(x, c, positions, ada_w, ada_b, norm_gains, w_in, conv_w, w_conv_out, cmp_pe, cmp_w1, cmp_b1, cmp_w2, cmp_b2,
           w_nsa_out, w_out, w_mlp_up, w_mlp_down):
    b, s, d = x.shape
    assert b == 1 and d == D_MODEL
    depth = ada_w.shape[0]
    cast = lambda w: w.astype(MXU_DTYPE)
    xs = x.reshape(s, d)
    mod = _adaln(c, ada_w, ada_b)
    tables = _rope_tables(positions)
    ks_col = COL_KV + 2 * KV_WIDTH
    for l in range(depth):
        sh1, sc1, g1, sh2, sc2, g2 = [mod[l, k * d:(k + 1) * d].reshape(1, d) for k in range(6)]
        gains = [norm_gains[l, k].reshape(1, d) for k in range(4)]
        wl = w_in[l]
        h = _norm_mod(xs, gains[0], sc1, sh1)
        cf = _conv_proj(h, cast(wl[:, :COL_Q]), conv_w[l])
        w_qkv = cast(jnp.concatenate([wl[:, COL_Q:COL_KV], wl[:, ks_col:COL_GATE]], axis=1))
        qkv = _head_proj(h, w_qkv, tables, (0, 1, 2, 3, 4, 6), (0, 1, 2, 3), MXU_DTYPE)
        kv_raw = _head_proj(h, cast(wl[:, COL_KV:ks_col]), tables, (0,), (), F32)
        gates = _gate_proj(h, _gate_weight(wl))
        kvc = _compress(kv_raw, cmp_pe[l], cast(cmp_w1[l]), cmp_b1[l], cast(cmp_w2[l]), cmp_b2[l])
        ocw, imp = _cmp_win(qkv, kvc, gates, s)
        attn = _sel_attention(qkv, _topk_bias(imp), ocw, gates, s)
        merged = _merge(h, cf, attn, cast(wl[:, COL_GA:COL_GB]), cast(wl[:, COL_GB:]), cast(w_conv_out[l]), cast(w_nsa_out[l]))
        xs = _out_proj(merged, cast(w_out[l]), xs, gains[1], g1)
        xs = _mlp(xs, gains[2], sc2, sh2, cast(w_mlp_up[l]), cast(w_mlp_down[l]), gains[3], g2)
    return xs.reshape(b, s, d)
```

```python
import functools

import jax
import jax.numpy as jnp
from jax import lax
from jax.experimental import pallas as pl
from jax.experimental.pallas import tpu as pltpu

F32 = jnp.float32
MXU_DTYPE = jnp.bfloat16

D_MODEL = 2048
CONV_WIDTH = D_MODEL // 2
N_HEADS = 16
N_KV_GROUPS = 4
HEADS_PER_GROUP = N_HEADS // N_KV_GROUPS
HEAD_DIM = D_MODEL // N_HEADS
KV_WIDTH = N_KV_GROUPS * HEAD_DIM
ROPE_DIM = HEAD_DIM // 4
ROPE_THETA = 500000.0
CMP_BLOCK = 32
CMP_STRIDE = 16
CMP_HIDDEN = 2 * HEAD_DIM
SEL_BLOCK = 64
SEL_SHIFT = 6
SEL_TOP = 16
WINDOW = 512
D_FF = 4 * D_MODEL
NORM_EPS = 1e-6
NEG = -1e30
REMOVED = -3e38
EXP2_SCALE = HEAD_DIM ** -0.5 * 1.4426950408889634

COL_Q = 3 * CONV_WIDTH
COL_KV = COL_Q + N_HEADS * HEAD_DIM
COL_GATE = COL_KV + 6 * KV_WIDTH
COL_GA = COL_GATE + 3 * N_HEADS
COL_GB = COL_GA + D_MODEL
GATES_PER_GROUP = 3 * HEADS_PER_GROUP

VMEM_LIMIT_BYTES = 56 * 1024 * 1024
LANES = 128
SUBLANES = 8
PACKED_ROWS = 16
MXU_WIDTH = 256
assert HEAD_DIM == LANES


def _params(*semantics):
    return pltpu.CompilerParams(dimension_semantics=semantics, vmem_limit_bytes=VMEM_LIMIT_BYTES)


def _dot(a, b):
    return jnp.dot(a, b, preferred_element_type=F32)


def _dot_nt(a, b):
    return lax.dot_general(a, b, (((1,), (1,)), ((), ())), preferred_element_type=F32)


def _rms(y, gain):
    ms = jnp.mean(y * y, axis=-1, keepdims=True)
    return y * lax.rsqrt(ms + NORM_EPS) * gain


def _adaln_kernel(c_ref, w_ref, b_ref, o_ref):
    c = c_ref[...]
    d, tn = w_ref.shape[1:]
    act = (c * jax.nn.sigmoid(c)).reshape(d // SUBLANES, SUBLANES, LANES)
    for t in range(tn // LANES):
        cols = slice(t * LANES, (t + 1) * LANES)
        part = jnp.sum(w_ref[0, :, cols].reshape(d // SUBLANES, SUBLANES, LANES) * act, axis=0)
        o_ref[0, :, cols] = jnp.sum(part, axis=0, keepdims=True) + b_ref[0, :, cols]


def _adaln(c, ada_w, ada_b):
    depth, d, n = ada_w.shape
    tn = 1024
    c_rows = jnp.broadcast_to(c.reshape(d, 1), (d, LANES))
    vec = pl.BlockSpec((1, 1, tn), lambda l, j: (l, 0, j))
    out = pl.pallas_call(
        _adaln_kernel,
        grid=(depth, n // tn),
        in_specs=[pl.BlockSpec((d, LANES), lambda l, j: (0, 0)), pl.BlockSpec((1, d, tn), lambda l, j: (l, 0, j)), vec],
        out_specs=vec,
        out_shape=jax.ShapeDtypeStruct((depth, 1, n), F32),
        compiler_params=_params("parallel", "parallel"),
        name="adaln",
    )(c_rows, ada_w, ada_b.reshape(depth, 1, n))
    return out[:, 0, :]


def _rope_table_kernel(pos_ref, freq_ref, c_ref, s1_ref, s2_ref):
    half = ROPE_DIM // 2
    ang = pos_ref[...].astype(F32) * freq_ref[...]
    lane = lax.broadcasted_iota(jnp.int32, ang.shape, 1)
    cos = jnp.cos(ang)
    sin = jnp.sin(ang)
    c_ref[...] = jnp.where(lane < ROPE_DIM, cos, 1.0)
    s1_ref[...] = jnp.where(lane < half, -sin, 0.0)
    s2_ref[...] = jnp.where((lane >= half) & (lane < ROPE_DIM), sin, 0.0)


def _rope_tables(positions):
    s = positions.shape[-1]
    tm = min(s, 2048)
    half = ROPE_DIM // 2
    inv_freq = ROPE_THETA ** (-jnp.arange(0, ROPE_DIM, 2, dtype=F32) / ROPE_DIM)
    freq = jnp.concatenate([inv_freq, inv_freq, jnp.zeros((HEAD_DIM - 2 * half,), F32)]).reshape(1, HEAD_DIM)
    spec = pl.BlockSpec((tm, HEAD_DIM), lambda i: (i, 0))
    shape = jax.ShapeDtypeStruct((s, HEAD_DIM), F32)
    return pl.pallas_call(
        _rope_table_kernel,
        grid=(s // tm,),
        in_specs=[pl.BlockSpec((tm, 1), lambda i: (i, 0)), pl.BlockSpec((1, HEAD_DIM), lambda i: (0, 0))],
        out_specs=[spec, spec, spec],
        out_shape=[shape, shape, shape],
        compiler_params=_params("parallel"),
        name="rope_tables",
    )(positions.reshape(s, 1), freq)


def _rope(x, c, s1, s2):
    half = ROPE_DIM // 2
    return x * c + pltpu.roll(x, HEAD_DIM - half, axis=1) * s1 + pltpu.roll(x, half, axis=1) * s2


def _norm_mod_kernel(x_ref, g_ref, sc_ref, sh_ref, o_ref):
    y = _rms(x_ref[...], g_ref[...])
    o_ref[...] = (y * (1.0 + sc_ref[...]) + sh_ref[...]).astype(o_ref.dtype)


def _norm_mod(x, gain, sc, sh):
    s, d = x.shape
    tm = min(s, 512)
    vec = pl.BlockSpec((1, d), lambda i: (0, 0))
    return pl.pallas_call(
        _norm_mod_kernel,
        grid=(s // tm,),
        in_specs=[pl.BlockSpec((tm, d), lambda i: (i, 0)), vec, vec, vec],
        out_specs=pl.BlockSpec((tm, d), lambda i: (i, 0)),
        out_shape=jax.ShapeDtypeStruct((s, d), MXU_DTYPE),
        compiler_params=_params("parallel"),
        name="norm_mod",
    )(x, gain, sc, sh)


def _conv_proj_kernel(h_ref, wb_ref, wc_ref, wx_ref, cw_ref, o_ref, carry_ref):
    i = pl.program_id(0)
    j = pl.program_id(1)
    tm = h_ref.shape[0]
    h = h_ref[...]
    u = _dot(h, wc_ref[...]) * _dot(h, wx_ref[...])

    @pl.when(i == 0)
    def _():
        carry_ref[j] = jnp.zeros(carry_ref.shape[1:], F32)

    prev = carry_ref[j]
    carry_ref[j] = u[tm - SUBLANES:, :]
    row = lax.broadcasted_iota(jnp.int32, u.shape, 0)
    last, before_last = prev[SUBLANES - 1:, :], prev[SUBLANES - 2:SUBLANES - 1, :]
    u1 = jnp.where(row == 0, last, pltpu.roll(u, 1, axis=0))
    u2 = jnp.where(row == 0, before_last, jnp.where(row == 1, last, pltpu.roll(u, 2, axis=0)))
    cw = cw_ref[...]
    z = cw[2:3, :] * u + cw[1:2, :] * u1 + cw[0:1, :] * u2
    o_ref[...] = (_dot(h, wb_ref[...]) * z).astype(o_ref.dtype)


def _conv_proj(h, w_conv, conv_w):
    s, d = h.shape
    tm = min(s, 1024)
    tn = 512
    nb = CONV_WIDTH // tn
    return pl.pallas_call(
        _conv_proj_kernel,
        grid=(s // tm, nb),
        in_specs=[
            pl.BlockSpec((tm, d), lambda i, j: (i, 0)),
            pl.BlockSpec((d, tn), lambda i, j: (0, j)),
            pl.BlockSpec((d, tn), lambda i, j: (0, j + nb)),
            pl.BlockSpec((d, tn), lambda i, j: (0, j + 2 * nb)),
            pl.BlockSpec((3, tn), lambda i, j: (0, j)),
        ],
        out_specs=pl.BlockSpec((tm, tn), lambda i, j: (i, j)),
        out_shape=jax.ShapeDtypeStruct((s, CONV_WIDTH), MXU_DTYPE),
        scratch_shapes=[pltpu.VMEM((nb, SUBLANES, tn), F32)],
        compiler_params=_params("arbitrary", "arbitrary"),
        name="conv_proj",
    )(h, w_conv, w_conv, w_conv, conv_w)


def _head_proj_kernel(h_ref, w_ref, c_ref, s1_ref, s2_ref, o_ref, *, rope_groups, query_groups):
    j = pl.program_id(1)
    groups_per_step = o_ref.shape[0] // HEADS_PER_GROUP
    width = HEADS_PER_GROUP * HEAD_DIM
    any_of = lambda group, members: functools.reduce(jnp.logical_or, [group == r for r in members], False)
    for k in range(groups_per_step):
        acc = _dot(h_ref[...], w_ref[:, k * width:(k + 1) * width])
        is_rope = any_of(j * groups_per_step + k, rope_groups)
        q_scale = jnp.where(any_of(j * groups_per_step + k, query_groups), EXP2_SCALE, 1.0)
        c = jnp.where(is_rope, c_ref[...], 1.0) * q_scale
        s1 = jnp.where(is_rope, s1_ref[...], 0.0) * q_scale
        s2 = jnp.where(is_rope, s2_ref[...], 0.0) * q_scale
        for r in range(HEADS_PER_GROUP):
            head = _rope(acc[:, r * HEAD_DIM:(r + 1) * HEAD_DIM], c, s1, s2)
            o_ref[k * HEADS_PER_GROUP + r] = head.astype(o_ref.dtype)


def _head_proj(h, w, tables, rope_groups, query_groups, out_dtype):
    s, d = h.shape
    tm = min(s, 512)
    tn = min(w.shape[1], 4 * HEADS_PER_GROUP * HEAD_DIM)
    heads_per_step = tn // HEAD_DIM
    tab = pl.BlockSpec((tm, HEAD_DIM), lambda i, j: (i, 0))
    return pl.pallas_call(
        functools.partial(_head_proj_kernel, rope_groups=rope_groups, query_groups=query_groups),
        grid=(s // tm, w.shape[1] // tn),
        in_specs=[pl.BlockSpec((tm, d), lambda i, j: (i, 0)), pl.BlockSpec((d, tn), lambda i, j: (0, j)), tab, tab, tab],
        out_specs=pl.BlockSpec((heads_per_step, tm, HEAD_DIM), lambda i, j: (j, i, 0)),
        out_shape=jax.ShapeDtypeStruct((w.shape[1] // HEAD_DIM, s, HEAD_DIM), out_dtype),
        compiler_params=_params("parallel", "parallel"),
        name="head_proj",
    )(h, w, *tables)


def _gate_proj_kernel(h_ref, w_ref, o_ref):
    o_ref[...] = jax.nn.sigmoid(_dot(h_ref[...], w_ref[...]))


def _gate_proj(h, w_gate):
    s, d = h.shape
    tm = min(s, 1024)
    n = w_gate.shape[1]
    return pl.pallas_call(
        _gate_proj_kernel,
        grid=(s // tm,),
        in_specs=[pl.BlockSpec((tm, d), lambda i: (i, 0)), pl.BlockSpec((d, n), lambda i: (0, 0))],
        out_specs=pl.BlockSpec((tm, n), lambda i: (i, 0)),
        out_shape=jax.ShapeDtypeStruct((s, n), F32),
        compiler_params=_params("parallel"),
        name="gate_proj",
    )(h, w_gate)


def _compress_kernel(x_ref, pe_ref, w1_ref, b1_ref, w2_ref, b2_ref, o_ref):
    n = o_ref.shape[1]
    k_lo = CMP_STRIDE * HEAD_DIM
    pe = pe_ref[0]
    lo, hi = [], []
    for l in range(CMP_STRIDE):
        rows = x_ref[0, pl.ds(l, n, stride=CMP_STRIDE), :]
        lo.append((rows + pe[l:l + 1, :]).astype(MXU_DTYPE))
        hi.append((rows + pe[CMP_STRIDE + l:CMP_STRIDE + l + 1, :]).astype(MXU_DTYPE))
    t_lo = _dot(jnp.concatenate(lo, axis=1), w1_ref[0, :k_lo, :])
    t_hi = _dot(jnp.concatenate(hi, axis=1), w1_ref[0, k_lo:, :])
    hid = t_lo + pltpu.roll(t_hi, n - 1, axis=0) + b1_ref[0]
    out = _dot(jax.nn.gelu(hid).astype(MXU_DTYPE), w2_ref[0]) + b2_ref[0]
    row = lax.broadcasted_iota(jnp.int32, out.shape, 0)
    o_ref[0] = jnp.where(row == n - 1, 0.0, out).astype(o_ref.dtype)


def _compress(kv_raw, pe, w1, b1, w2, b2):
    nh, s, dh = kv_raw.shape
    n = s // CMP_STRIDE
    g = N_KV_GROUPS
    return pl.pallas_call(
        _compress_kernel,
        grid=(nh,),
        in_specs=[
            pl.BlockSpec((1, s, dh), lambda a: (a, 0, 0)),
            pl.BlockSpec((1, CMP_BLOCK, dh), lambda a: (a // g, 0, 0)),
            pl.BlockSpec((1, CMP_BLOCK * dh, CMP_HIDDEN), lambda a: (a // g, 0, 0)),
            pl.BlockSpec((1, 1, CMP_HIDDEN), lambda a: (a // g, 0, 0)),
            pl.BlockSpec((1, CMP_HIDDEN, dh), lambda a: (a // g, 0, 0)),
            pl.BlockSpec((1, 1, dh), lambda a: (a // g, 0, 0)),
        ],
        out_specs=pl.BlockSpec((1, n, dh), lambda a: (a, 0, 0)),
        out_shape=jax.ShapeDtypeStruct((nh, n, dh), MXU_DTYPE),
        compiler_params=_params("parallel"),
        name="compress",
    )(kv_raw, pe, w1, b1.reshape(2, 1, CMP_HIDDEN), w2, b2.reshape(2, 1, dh))


def _softmax_numerators(s, bias):
    sm = s + bias
    m = jnp.max(sm, axis=-1, keepdims=True)
    m = jnp.where(m < 0.5 * NEG, 0.0, m)
    return jnp.exp2(sm - m).astype(MXU_DTYPE)


def _cmp_win_kernel(q_ref, kc_ref, vc_ref, kw2_ref, kw1_ref, kw0_ref, vw2_ref, vw1_ref, vw0_ref, ov_ref, band_ref, gate_ref,
                    ocw_ref, imp_ref, *, tq, chunk):
    n_cmp = kc_ref.shape[1]
    last_visible = ((pl.program_id(1) + 1) * tq - CMP_BLOCK) // CMP_STRIDE
    for c in range(n_cmp // chunk):
        pl.when(jnp.maximum(last_visible, 0) // chunk == c)(functools.partial(
            _cmp_win_body, q_ref, kc_ref, vc_ref, kw2_ref, kw1_ref, kw0_ref, vw2_ref, vw1_ref, vw0_ref, ov_ref, band_ref, gate_ref,
            ocw_ref, imp_ref, tq=tq, n_cmp=(c + 1) * chunk))


def _cmp_win_body(q_ref, kc_ref, vc_ref, kw2_ref, kw1_ref, kw0_ref, vw2_ref, vw1_ref, vw0_ref, ov_ref, band_ref, gate_ref,
                  ocw_ref, imp_ref, *, tq, n_cmp):
    i = pl.program_id(1)
    r_heads = HEADS_PER_GROUP
    s0 = i * tq
    q4 = q_ref[...].reshape(r_heads * tq, HEAD_DIM)
    tok = s0 + lax.broadcasted_iota(jnp.int32, (tq, 1), 0)

    kwin = jnp.concatenate([kw2_ref[0], kw1_ref[0], kw0_ref[0]], axis=0)
    s_w = _dot_nt(q4, kwin)
    s_c = _dot_nt(q4, kc_ref[0, :n_cmp, :])
    vwin = jnp.concatenate([vw2_ref[0], vw1_ref[0], vw0_ref[0]], axis=0)
    wpos = s0 - 2 * tq + lax.broadcasted_iota(jnp.int32, (1, 3 * tq), 1)
    bias_w = jnp.where(wpos >= 0, band_ref[...], NEG)
    p_w = jnp.concatenate([_softmax_numerators(s_w[r * tq:(r + 1) * tq, :], bias_w) for r in range(r_heads)], axis=0)
    res_w = _dot(p_w, jnp.concatenate([vwin, jnp.ones((3 * tq, HEAD_DIM), MXU_DTYPE)], axis=1))
    cmp_end = lax.broadcasted_iota(jnp.int32, (1, n_cmp), 1) * CMP_STRIDE + (CMP_BLOCK - 1)
    bias_c = jnp.where(cmp_end <= tok, 0.0, NEG)
    p_c = jnp.concatenate([_softmax_numerators(s_c[r * tq:(r + 1) * tq, :], bias_c) for r in range(r_heads)], axis=0)
    ones_c = jnp.ones((n_cmp, HEAD_DIM), MXU_DTYPE)
    res_c = _dot(p_c, jnp.concatenate([vc_ref[0, :n_cmp, :], ones_c, ov_ref[:n_cmp, :]], axis=1))
    o_c, imp = [], None
    for r in range(r_heads):
        rows = slice(r * tq, (r + 1) * tq)
        l = res_c[rows, HEAD_DIM:2 * HEAD_DIM]
        inv = jnp.where(l > 0.0, 1.0 / l, 0.0)
        o_c.append(res_c[rows, :HEAD_DIM] * inv)
        part = res_c[rows, 2 * HEAD_DIM:] * inv[:, :1]
        imp = part if imp is None else imp + part
    imp_ref[0] = imp

    gate = gate_ref[...]
    for r in range(r_heads):
        rows = slice(r * tq, (r + 1) * tq)
        g_cmp = gate[:, 3 * r:3 * r + 1]
        g_win = gate[:, 3 * r + 2:3 * r + 3]
        o_w = res_w[rows, :HEAD_DIM] / res_w[rows, HEAD_DIM:]
        ocw_ref[:, r * HEAD_DIM:(r + 1) * HEAD_DIM] = g_cmp * o_c[r] + g_win * o_w


def _overlap_matrix(n_cmp_rows, n_slc):
    cs = jnp.arange(n_cmp_rows)[:, None] * CMP_STRIDE
    ss = jnp.arange(n_slc)[None, :] * SEL_BLOCK
    ov = jnp.clip(jnp.minimum(cs + CMP_BLOCK, ss + SEL_BLOCK) - jnp.maximum(cs, ss), 0, None)
    return (ov.astype(F32) / CMP_BLOCK).astype(MXU_DTYPE)


def _window_band(tq):
    diff = jnp.arange(tq)[:, None] + 2 * tq - jnp.arange(3 * tq)[None, :]
    return jnp.where((diff >= 0) & (diff < WINDOW), 0.0, NEG).astype(F32)


def _cmp_win(qkv, kvc, gates, s):
    tq = min(s, 256)
    assert WINDOW <= 2 * tq
    g = N_KV_GROUPS
    n_cmp = kvc.shape[1]
    n_slc = s // SEL_BLOCK
    head = lambda base: [
        pl.BlockSpec((1, tq, HEAD_DIM), functools.partial(lambda a, i, d, base: (base + a, jnp.maximum(i - d, 0), 0), d=d, base=base))
        for d in (2, 1, 0)
    ]
    ocw, imp = pl.pallas_call(
        functools.partial(_cmp_win_kernel, tq=tq, chunk=min(n_cmp, MXU_WIDTH)),
        grid=(g, s // tq),
        in_specs=[
            pl.BlockSpec((HEADS_PER_GROUP, tq, HEAD_DIM), lambda a, i: (a, i, 0)),
            pl.BlockSpec((1, n_cmp, HEAD_DIM), lambda a, i: (a, 0, 0)),
            pl.BlockSpec((1, n_cmp, HEAD_DIM), lambda a, i: (g + a, 0, 0)),
            *head(N_HEADS + 2 * g),
            *head(N_HEADS + 3 * g),
            pl.BlockSpec((n_cmp, n_slc), lambda a, i: (0, 0)),
            pl.BlockSpec((tq, 3 * tq), lambda a, i: (0, 0)),
            pl.BlockSpec((tq, HEAD_DIM), lambda a, i: (i, a)),
        ],
        out_specs=[
            pl.BlockSpec((tq, HEADS_PER_GROUP * HEAD_DIM), lambda a, i: (i, a)),
            pl.BlockSpec((1, tq, n_slc), lambda a, i: (a, i, 0)),
        ],
        out_shape=[
            jax.ShapeDtypeStruct((s, N_HEADS * HEAD_DIM), F32),
            jax.ShapeDtypeStruct((g, s, n_slc), F32),
        ],
        compiler_params=_params("parallel", "parallel"),
        name="cmp_win",
    )(qkv, kvc, kvc, qkv, qkv, qkv, qkv, qkv, qkv, _overlap_matrix(n_cmp, n_slc), _window_band(tq), gates)
    return ocw, imp


def _topk_kernel(imp_ref, sb_ref, work_ref):
    tm, n_slc = work_ref.shape
    tok = pl.program_id(1) * tm + lax.broadcasted_iota(jnp.int32, (tm, 1), 0)
    blk = lax.broadcasted_iota(jnp.int32, (1, n_slc), 1)
    cur = tok >> SEL_SHIFT
    forced = (blk == cur) | (blk == 0)
    candidate = (blk <= cur) & jnp.logical_not(forced)
    n_pick = min(SEL_TOP, n_slc) - 2
    start = lambda: jnp.where(forced, REMOVED, jnp.where(blk <= cur, imp_ref[0], NEG))

    work_ref[...] = start()
    for _ in range(n_pick):
        w = work_ref[...]
        work_ref[...] = jnp.where(w == jnp.max(w, axis=-1, keepdims=True), REMOVED, w)
    taken = jnp.sum(jnp.where((work_ref[...] == REMOVED) & candidate, 1.0, 0.0))
    expected = jnp.sum(jnp.minimum(jnp.maximum(cur - 1, 0), n_pick).astype(F32))

    @pl.when(taken > expected)
    def _():
        blk_f = blk.astype(F32)
        work_ref[...] = start()
        for _ in range(n_pick):
            w = work_ref[...]
            best = jnp.max(w, axis=-1, keepdims=True)
            idx = jnp.min(jnp.where(w == best, blk_f, float(n_slc)), axis=-1, keepdims=True)
            work_ref[...] = jnp.where(blk_f == idx, REMOVED, w)

    sb_ref[0] = jnp.where(work_ref[...] == REMOVED, 0.0, NEG).astype(sb_ref.dtype)


def _topk_bias(imp):
    g, s, n_slc = imp.shape
    tm = min(s, 1024)
    spec = pl.BlockSpec((1, tm, n_slc), lambda a, i: (a, i, 0))
    return pl.pallas_call(
        _topk_kernel,
        grid=(g, s // tm),
        in_specs=[spec],
        out_specs=spec,
        out_shape=jax.ShapeDtypeStruct((g, s, n_slc), MXU_DTYPE),
        scratch_shapes=[pltpu.VMEM((tm, n_slc), F32)],
        compiler_params=_params("parallel", "parallel"),
        name="topk_bias",
    )(imp)


def _sel_kernel(q_ref, k_ref, v_ref, sb_ref, ocw_ref, gate_ref, o_ref, lhs_ref, s_ref, m_ref, acc_ref, *, tq, tk):
    i = pl.program_id(1)
    r_heads = HEADS_PER_GROUP
    bias_lanes = lhs_ref.shape[2] - HEAD_DIM
    blocks_per_tile = tk // SEL_BLOCK
    lane_tiles = tk // LANES
    n_tiles = (i * tq + tq - 1) // tk + 1
    tok = i * tq + lax.broadcasted_iota(jnp.int32, (tq, 1), 0)
    key_blk = lax.broadcasted_iota(jnp.int32, (tk, bias_lanes), 0) >> SEL_SHIFT
    lane = lax.broadcasted_iota(jnp.int32, (tk, bias_lanes), 1)
    key_off = lax.broadcasted_iota(jnp.int32, (1, tk), 1)

    for c in range(lhs_ref.shape[0]):
        for r in range(r_heads):
            rows = slice(r * tq, (r + 1) * tq)
            lhs_ref[c, rows, :HEAD_DIM] = q_ref[r]
            lhs_ref[c, rows, HEAD_DIM:] = sb_ref[0, :, c * bias_lanes:(c + 1) * bias_lanes]

    def produce(slot, j, parts=1):
        start = pl.multiple_of(j * tk, tk)
        first_blk = j * blocks_per_tile
        onehot = jnp.where(lane == first_blk % bias_lanes + key_blk, 1.0, 0.0).astype(MXU_DTYPE)
        k_ext = jnp.concatenate([k_ref[0, pl.ds(start, tk), :], onehot], axis=1)
        n = r_heads * tq // parts
        for h in range(parts):
            s_ref[slot, h * n:(h + 1) * n, :] = _dot_nt(lhs_ref[first_blk // bias_lanes, h * n:(h + 1) * n, :], k_ext)

    def consume(slot, j, diagonal, parts=1):
        start = pl.multiple_of(j * tk, tk)
        v_ext = jnp.concatenate([v_ref[0, pl.ds(start, tk), :], jnp.ones((tk, HEAD_DIM), MXU_DTYPE)], axis=1)
        p_rows, alphas = [], []
        for r in range(r_heads):
            rows = slice(r * tq, (r + 1) * tq)
            sm = s_ref[slot, rows, :]
            if diagonal:
                sm = jnp.where(j * tk + key_off <= tok, sm, NEG)
            m_prev = m_ref[rows, :]
            m_new = jnp.maximum(m_prev, jnp.max(sm, axis=-1, keepdims=True))
            m_ref[rows, :] = m_new
            p_rows.append(jnp.exp2(sm - jnp.concatenate([m_new] * lane_tiles, axis=1)).astype(MXU_DTYPE))
            alphas.append(jnp.exp2(m_prev - m_new))
        per = r_heads // parts
        for h in range(parts):
            pv = _dot(jnp.concatenate(p_rows[h * per:(h + 1) * per], axis=0), v_ext)
            for r in range(per):
                rows = slice((h * per + r) * tq, (h * per + r + 1) * tq)
                scale = jnp.concatenate([alphas[h * per + r]] * 2, axis=1)
                acc_ref[rows, :] = acc_ref[rows, :] * scale + pv[r * tq:(r + 1) * tq, :]

    produce(0, 0, parts=2)
    m_ref[...] = jnp.full(m_ref.shape, NEG, F32)
    acc_ref[...] = jnp.zeros(acc_ref.shape, F32)

    def body(pair, carry):
        j = 2 * pair
        consume(0, j, diagonal=False)
        produce(1, j + 1)
        consume(1, j + 1, diagonal=False)
        produce(0, j + 2)
        return carry

    n_pairs = (n_tiles - 1) // 2
    lax.fori_loop(0, n_pairs, body, 0)
    done = 2 * n_pairs

    @pl.when(n_tiles - done == 1)
    def _():
        consume(0, done, diagonal=True, parts=2)

    @pl.when(n_tiles - done == 2)
    def _():
        consume(0, done, diagonal=False)
        produce(1, done + 1)
        consume(1, done + 1, diagonal=True, parts=2)

    gate = gate_ref[...]
    for r in range(r_heads):
        rows = slice(r * tq, (r + 1) * tq)
        cols = slice(r * HEAD_DIM, (r + 1) * HEAD_DIM)
        o_s = acc_ref[rows, :HEAD_DIM] / acc_ref[rows, HEAD_DIM:]
        o_ref[:, cols] = (ocw_ref[:, cols] + gate[:, 3 * r + 1:3 * r + 2] * o_s).astype(o_ref.dtype)


def _sel_attention(qkv, sel_bias, ocw, gates, s):
    tq = min(s, 512)
    tk = min(s, 512)
    g = N_KV_GROUPS
    n_slc = s // SEL_BLOCK
    width = HEADS_PER_GROUP * HEAD_DIM
    rows = HEADS_PER_GROUP * tq
    bias_lanes = min(LANES, n_slc)
    assert n_slc % bias_lanes == 0 and bias_lanes % (tk // SEL_BLOCK) == 0 and tk % tq == 0
    return pl.pallas_call(
        functools.partial(_sel_kernel, tq=tq, tk=tk),
        grid=(g, s // tq),
        in_specs=[
            pl.BlockSpec((HEADS_PER_GROUP, tq, HEAD_DIM), lambda a, i: (a, i, 0)),
            pl.BlockSpec((1, s, HEAD_DIM), lambda a, i: (N_HEADS + a, 0, 0)),
            pl.BlockSpec((1, s, HEAD_DIM), lambda a, i: (N_HEADS + g + a, 0, 0)),
            pl.BlockSpec((1, tq, n_slc), lambda a, i: (a, i, 0)),
            pl.BlockSpec((tq, width), lambda a, i: (i, a)),
            pl.BlockSpec((tq, HEAD_DIM), lambda a, i: (i, a)),
        ],
        out_specs=pl.BlockSpec((tq, width), lambda a, i: (i, a)),
        out_shape=jax.ShapeDtypeStruct((s, N_HEADS * HEAD_DIM), MXU_DTYPE),
        scratch_shapes=[
            pltpu.VMEM((n_slc // bias_lanes, rows, HEAD_DIM + bias_lanes), MXU_DTYPE),
            pltpu.VMEM((2, rows, tk), F32),
            pltpu.VMEM((rows, HEAD_DIM), F32),
            pltpu.VMEM((rows, 2 * HEAD_DIM), F32),
        ],
        compiler_params=_params("parallel", "arbitrary"),
        name="sel_attention",
    )(qkv, qkv, qkv, sel_bias, ocw, gates)


def _merge_kernel(h_ref, cf_ref, at_ref, wga_ref, wgb_ref, wco_ref, wno_ref, o_ref):
    h = h_ref[...]
    y_a = _dot(cf_ref[...], wco_ref[...])
    y_b = _dot(at_ref[...], wno_ref[...])
    merged = jax.nn.sigmoid(_dot(h, wga_ref[...])) * y_a + jax.nn.sigmoid(_dot(h, wgb_ref[...])) * y_b
    o_ref[...] = merged.astype(o_ref.dtype)


def _merge(h, cf, attn, w_ga, w_gb, w_co, w_no):
    s, d = h.shape
    tm = min(s, 1024)
    tn = 512
    row = lambda width: pl.BlockSpec((tm, width), lambda i, j: (i, 0))
    col = lambda depth: pl.BlockSpec((depth, tn), lambda i, j: (0, j))
    return pl.pallas_call(
        _merge_kernel,
        grid=(s // tm, d // tn),
        in_specs=[row(d), row(CONV_WIDTH), row(N_HEADS * HEAD_DIM), col(d), col(d), col(CONV_WIDTH), col(N_HEADS * HEAD_DIM)],
        out_specs=pl.BlockSpec((tm, tn), lambda i, j: (i, j)),
        out_shape=jax.ShapeDtypeStruct((s, d), MXU_DTYPE),
        compiler_params=_params("parallel", "parallel"),
        name="merge",
    )(h, cf, attn, w_ga, w_gb, w_co, w_no)


def _out_proj_kernel(m_ref, w_ref, x_ref, gain_ref, g_ref, o_ref):
    y = _dot(m_ref[...], w_ref[...])
    o_ref[...] = x_ref[...] + g_ref[...] * _rms(y, gain_ref[...])


def _out_proj(merged, w_out, x, gain, g1):
    s, d = x.shape
    tm = min(s, 512)
    vec = pl.BlockSpec((1, d), lambda i: (0, 0))
    row = pl.BlockSpec((tm, d), lambda i: (i, 0))
    return pl.pallas_call(
        _out_proj_kernel,
        grid=(s // tm,),
        in_specs=[row, pl.BlockSpec((d, d), lambda i: (0, 0)), row, vec, vec],
        out_specs=row,
        out_shape=jax.ShapeDtypeStruct((s, d), F32),
        compiler_params=_params("parallel"),
        name="out_proj",
    )(merged, w_out, x, gain, g1)


def _mlp_kernel(x_ref, xn_ref, gin_ref, sc_ref, sh_ref, wu_ref, wd_ref, gout_ref, g_ref, o_ref, h0_ref, h1_ref, acc_ref,
                *, n_f):
    i = pl.program_id(0)
    f = pl.program_id(1)
    chunk = h0_ref.shape[0] // n_f

    def pre_norm(x):
        return (_rms(x, gin_ref[...]) * (1.0 + sc_ref[...]) + sh_ref[...]).astype(h0_ref.dtype)

    @pl.when((i == 0) & (f == 0))
    def _():
        h0_ref[...] = pre_norm(x_ref[...])

    @pl.when(f == 0)
    def _():
        acc_ref[...] = jnp.zeros(acc_ref.shape, F32)

    def step(h_ref, h_next_ref):
        r0 = pl.multiple_of(f * chunk, chunk)
        h_next_ref[pl.ds(r0, chunk), :] = pre_norm(xn_ref[pl.ds(r0, chunk), :])
        u = jnp.square(jnp.maximum(_dot(h_ref[...], wu_ref[...]), 0.0))
        acc_ref[...] += _dot(u.astype(MXU_DTYPE), wd_ref[...])

    pl.when(i % 2 == 0)(functools.partial(step, h0_ref, h1_ref))
    pl.when(i % 2 == 1)(functools.partial(step, h1_ref, h0_ref))

    @pl.when(f == pl.num_programs(1) - 1)
    def _():
        o_ref[...] = x_ref[...] + g_ref[...] * _rms(acc_ref[...], gout_ref[...])


def _mlp(x, gain_in, sc, sh, w_up, w_down, gain_out, g2):
    s, d = x.shape
    tm = min(s, 512)
    tf = 1024
    n_i, n_f = s // tm, D_FF // tf
    assert tm % (PACKED_ROWS * n_f) == 0
    vec = pl.BlockSpec((1, d), lambda i, f: (0, 0))
    row = pl.BlockSpec((tm, d), lambda i, f: (i, 0))
    next_row = pl.BlockSpec((tm, d), lambda i, f: (jnp.minimum(i + 1, n_i - 1), 0))
    return pl.pallas_call(
        functools.partial(_mlp_kernel, n_f=n_f),
        grid=(n_i, n_f),
        in_specs=[row, next_row, vec, vec, vec, pl.BlockSpec((d, tf), lambda i, f: (0, f)),
                  pl.BlockSpec((tf, d), lambda i, f: (f, 0)), vec, vec],
        out_specs=row,
        out_shape=jax.ShapeDtypeStruct((s, d), F32),
        scratch_shapes=[pltpu.VMEM((tm, d), MXU_DTYPE), pltpu.VMEM((tm, d), MXU_DTYPE), pltpu.VMEM((tm, d), F32)],
        compiler_params=_params("arbitrary", "arbitrary"),
        name="mlp",
    )(x, x, gain_in, sc, sh, w_up, w_down, gain_out, g2)


def _gate_weight(w_in_l):
    w = w_in_l[:, COL_GATE:COL_GA].reshape(D_MODEL, N_KV_GROUPS, GATES_PER_GROUP)
    w = jnp.pad(w, ((0, 0), (0, 0), (0, HEAD_DIM - GATES_PER_GROUP)))
    return w.reshape(D_MODEL, N_KV_GROUPS * HEAD_DIM).astype(MXU_DTYPE)


def kernel(x, c, positions, ada_w, ada_b, norm_gains, w_in, conv_w, w_conv_out, cmp_pe, cmp_w1, cmp_b1, cmp_w2, cmp_b2,
           w_nsa_out, w_out, w_mlp_up, w_mlp_down):
    b, s, d = x.shape
    assert b == 1 and d == D_MODEL
    depth = ada_w.shape[0]
    cast = lambda w: w.astype(MXU_DTYPE)
    xs = x.reshape(s, d)
    mod = _adaln(c, ada_w, ada_b)
    tables = _rope_tables(positions)
    ks_col = COL_KV + 2 * KV_WIDTH
    for l in range(depth):
        sh1, sc1, g1, sh2, sc2, g2 = [mod[l, k * d:(k + 1) * d].reshape(1, d) for k in range(6)]
        gains = [norm_gains[l, k].reshape(1, d) for k in range(4)]
        wl = w_in[l]
        h = _norm_mod(xs, gains[0], sc1, sh1)
        cf = _conv_proj(h, cast(wl[:, :COL_Q]), conv_w[l])
        w_qkv = cast(jnp.concatenate([wl[:, COL_Q:COL_KV], wl[:, ks_col:COL_GATE]], axis=1))
        qkv = _head_proj(h, w_qkv, tables, (0, 1, 2, 3, 4, 6), (0, 1, 2, 3), MXU_DTYPE)
        kv_raw = _head_proj(h, cast(wl[:, COL_KV:ks_col]), tables, (0,), (), F32)
        gates = _gate_proj(h, _gate_weight(wl))
        kvc = _compress(kv_raw, cmp_pe[l], cast(cmp_w1[l]), cmp_b1[l], cast(cmp_w2[l]), cmp_b2[l])
        ocw, imp = _cmp_win(qkv, kvc, gates, s)
        attn = _sel_attention(qkv, _topk_bias(imp), ocw, gates, s)
        merged = _merge(h, cf, attn, cast(wl[:, COL_GA:COL_GB]), cast(wl[:, COL_GB:]), cast(w_conv_out[l]), cast(w_nsa_out[l]))
        xs = _out_proj(merged, cast(w_out[l]), xs, gains[1], g1)
        xs = _mlp(xs, gains[2], sc2, sh2, cast(w_mlp_up[l]), cast(w_mlp_down[l]), gains[3], g2)
    return xs.reshape(b, s, d)
```
